```python
import math
import jax, jax.numpy as jnp
from jax import lax
import numpy as np

D_MODEL = 1024
BATCH = 4
SEQ = 4096
DEPTH = 2
DEC_BATCH = 32
DEC_SEQ = 4
PAST_LEN = 16384
PAGE_SIZE = 128

FOX_HEADS = D_MODEL // 128
FOX_HEAD_DIM = 64
FOX_W = FOX_HEADS * FOX_HEAD_DIM
SSD_INNER = D_MODEL
SSD_HEAD_DIM = 64
SSD_HEADS = SSD_INNER // SSD_HEAD_DIM
SSD_GROUPS = 4
SSD_STATE = 128
SSD_CONV = 4
SSD_CONV_DIM = SSD_INNER + 2 * SSD_GROUPS * SSD_STATE
SSD_CHUNK = 128
DIFF_HEADS = D_MODEL // 256
DIFF_HEAD_DIM = 64
DIFF_W = DIFF_HEADS * 2 * DIFF_HEAD_DIM
PEER_HEADS = 8
PEER_KEYS = 128
PEER_EXPERTS = PEER_KEYS * PEER_KEYS
PEER_TOPK = 16
PEER_QDIM = 256
PEER_BLOCK = 128
N_BRANCH = 3
Q_BLOCK = 128
ROPE_THETA = 10000.0
EPS = 1e-6
IN_SPLITS = (FOX_W, FOX_W, FOX_W, FOX_HEADS, SSD_INNER, SSD_CONV_DIM, SSD_HEADS,
             DIFF_W, DIFF_W, DIFF_W, N_BRANCH * D_MODEL)
IN_COLS = sum(IN_SPLITS)
IN_OFFSETS = tuple(int(v) for v in np.cumsum(IN_SPLITS)[:-1])

kernel_name = 'fox_ssd_diffattn_peer_hybrid_step'


def _rms(x, g):
    xf = x.astype(jnp.float32)
    y = xf * lax.rsqrt(jnp.mean(xf * xf, axis=-1, keepdims=True) + EPS)
    return (y * g.astype(jnp.float32)).astype(x.dtype)


def _rope(x, pos):
    half = x.shape[-1] // 2
    inv = ROPE_THETA ** (-jnp.arange(half, dtype=jnp.float32) / half)
    ang = pos.astype(jnp.float32)[:, None] * inv[None, :]
    shp = (1, x.shape[1]) + (1,) * (x.ndim - 3) + (half,)
    cos, sin = jnp.cos(ang).reshape(shp), jnp.sin(ang).reshape(shp)
    xf = x.astype(jnp.float32)
    x1, x2 = xf[..., :half], xf[..., half:]
    return jnp.concatenate([x1 * cos - x2 * sin, x2 * cos + x1 * sin], axis=-1).astype(x.dtype)


def _q_block_size(sq):
    return Q_BLOCK if sq % Q_BLOCK == 0 else sq


def _to_blocks(a, qb):
    nb = a.shape[1] // qb
    return jnp.moveaxis(a.reshape((a.shape[0], nb, qb) + a.shape[2:]), 1, 0)


def _from_blocks(o):
    o = jnp.moveaxis(o, 0, 1)
    return o.reshape((o.shape[0], o.shape[1] * o.shape[2]) + o.shape[3:])


def _fox_attention(q, k, v, fq, fk, q_pos, k_pos):
    qb = _q_block_size(q.shape[1])
    scale = FOX_HEAD_DIM ** -0.5
    fk_t = jnp.swapaxes(fk, 1, 2)[:, :, None, :]

    def block(args):
        q_b, fq_b, p_b = args
        s = jnp.einsum('bqhd,bkhd->bhqk', q_b, k).astype(jnp.float32) * scale
        s = s + jnp.swapaxes(fq_b, 1, 2)[..., None] - fk_t
        s = jnp.where(k_pos[None, :] <= p_b[:, None], s, -jnp.inf)
        p = jax.nn.softmax(s, axis=-1).astype(v.dtype)
        return jnp.einsum('bhqk,bkhd->bqhd', p, v)

    out = lax.map(block, (_to_blocks(q, qb), _to_blocks(fq, qb), q_pos.reshape(-1, qb)))
    return _from_blocks(out)


def _diff_attention(q, k, v, lam, q_pos, k_pos):
    qb = _q_block_size(q.shape[1])
    scale = DIFF_HEAD_DIM ** -0.5

    def block(args):
        q_b, p_b = args
        s = jnp.einsum('bqhid,bkhid->bihqk', q_b, k).astype(jnp.float32) * scale
        s = jnp.where(k_pos[None, :] <= p_b[:, None], s, -jnp.inf)
        p = jax.nn.softmax(s, axis=-1)
        a = (p[:, 0] - lam * p[:, 1]).astype(v.dtype)
        return jnp.einsum('bhqk,bkhe->bqhe', a, v)

    out = lax.map(block, (_to_blocks(q, qb), q_pos.reshape(-1, qb)))
    return _from_blocks(out)


def _segsum(a):
    T = a.shape[-1]
    ar = jnp.broadcast_to(a[..., :, None], a.shape + (T,))
    strict = jnp.tril(jnp.ones((T, T), bool), -1)
    cs = jnp.cumsum(jnp.where(strict, ar, 0.0), axis=-2)
    return jnp.where(jnp.tril(jnp.ones((T, T), bool)), cs, -jnp.inf)


def _ssd(x, dt, A, Bm, Cm, s0):
    f32 = jnp.float32
    b, S, h, p = x.shape
    g, n = Bm.shape[2], Bm.shape[3]
    r = h // g
    L = SSD_CHUNK if S % SSD_CHUNK == 0 else S
    c = S // L
    xd = (x.astype(f32) * dt[..., None]).reshape(b, c, L, h, p)
    a = jnp.moveaxis((dt * A).reshape(b, c, L, h), 3, 1)
    Bc = Bm.astype(f32).reshape(b, c, L, g, n)
    Cc = Cm.astype(f32).reshape(b, c, L, g, n)
    a_cum = jnp.cumsum(a, axis=-1)
    decay = jnp.moveaxis(jnp.exp(_segsum(a)), 1, 2)
    cb = jnp.repeat(jnp.einsum('bclgn,bcsgn->bcgls', Cc, Bc), r, axis=2)
    y_diag = jnp.einsum('bchls,bcshp->bclhp', cb * decay, xd)
    Bh = jnp.repeat(Bc, r, axis=3)
    Ch = jnp.repeat(Cc, r, axis=3)
    to_end = jnp.exp(a_cum[..., -1:] - a_cum)
    chunk_states = jnp.einsum('bclhn,bhcl,bclhp->bchpn', Bh, to_end, xd)
    states = jnp.concatenate([s0.astype(f32)[:, None], chunk_states], axis=1)
    chunk_decay = jnp.exp(_segsum(jnp.pad(a_cum[..., -1], ((0, 0), (0, 0), (1, 0)))))
    states = jnp.einsum('bhzc,bchpn->bzhpn', chunk_decay, states)
    y_off = jnp.einsum('bclhn,bchpn,bhcl->bclhp', Ch, states[:, :-1], jnp.exp(a_cum))
    return (y_diag + y_off).reshape(b, S, h, p), states[:, -1]


def _peer(h, wq, sub_keys, u_tab, v_tab):
    B, S, D = h.shape
    T = B * S
    pad = (-T) % PEER_BLOCK
    ht = jnp.pad(h.reshape(T, D), ((0, pad), (0, 0)))
    blocks = ht.reshape(-1, PEER_BLOCK, D)

    def block(hb):
        q = (hb @ wq).reshape(PEER_BLOCK, PEER_HEADS, 2, PEER_QDIM // 2)
        s = jnp.einsum('thid,ikd->thik', q, sub_keys).astype(jnp.float32)
        top_v, top_i = lax.top_k(s, PEER_TOPK)
        cand = top_v[:, :, 0, :, None] + top_v[:, :, 1, None, :]
        best_v, best_c = lax.top_k(cand.reshape(PEER_BLOCK, PEER_HEADS, PEER_TOPK * PEER_TOPK), PEER_TOPK)
        i1 = jnp.take_along_axis(top_i[:, :, 0], best_c // PEER_TOPK, axis=-1)
        i2 = jnp.take_along_axis(top_i[:, :, 1], best_c % PEER_TOPK, axis=-1)
        e = i1 * PEER_KEYS + i2
        gw = jax.nn.softmax(best_v, axis=-1)
        a = jnp.einsum('thkd,td->thk', u_tab[e], hb).astype(jnp.float32)
        w = (gw * jax.nn.gelu(a, approximate=False)).astype(hb.dtype)
        return jnp.einsum('thk,thkd->td', w, v_tab[e])

    out = lax.map(block, blocks).reshape(-1, D)[:T]
    return out.reshape(B, S, D)


def _gather_pages(pool, layer, page_table):
    g = pool[layer, page_table]
    return g.reshape((g.shape[0], g.shape[1] * g.shape[2]) + g.shape[3:])


def _layer(x, cond, lp, layer_idx, fk_past, fv_past, ff_past, dk_past, dv_past, ssm0, conv0):
    f32 = jnp.float32
    B, S, D = x.shape
    P0 = fk_past.shape[1]
    q_pos = P0 + jnp.arange(S, dtype=jnp.int32)
    k_pos = jnp.arange(P0 + S, dtype=jnp.int32)

    mod = jax.nn.silu(cond) @ lp['ada_w'] + lp['ada_b']
    sh1, sc1, g1, sh2, sc2, g2 = jnp.split(mod[:, None, :], 6, axis=-1)

    h = _rms(x, lp['norm1_g']) * (1 + sc1) + sh1
    proj = h @ lp['w_in']
    fq, fk, fv, ff, z, xbc, dt, dq, dk, dv, gates = jnp.split(proj, IN_OFFSETS, axis=-1)

    q_f = _rms(fq.reshape(B, S, FOX_HEADS, FOX_HEAD_DIM), lp['fox_qn_g'])
    k_f = _rms(fk.reshape(B, S, FOX_HEADS, FOX_HEAD_DIM), lp['fox_kn_g'])
    v_f = fv.reshape(B, S, FOX_HEADS, FOX_HEAD_DIM)
    logf = jax.nn.log_sigmoid(ff.astype(f32) + lp['fox_f_b'].astype(f32))
    F = jnp.cumsum(jnp.concatenate([ff_past.astype(f32), logf], axis=1), axis=1)
    o_fox = _fox_attention(q_f, jnp.concatenate([fk_past, k_f], axis=1),
                           jnp.concatenate([fv_past, v_f], axis=1), F[:, P0:], F, q_pos, k_pos)
    o_fox = o_fox.reshape(B, S, FOX_W)

    xbc_all = jnp.concatenate([conv0.astype(xbc.dtype), xbc], axis=1)
    xbc_c = lax.conv_general_dilated(xbc_all, lp['conv_w'][:, None, :].astype(xbc.dtype), (1,), 'VALID',
                                     dimension_numbers=('NWC', 'WIO', 'NWC'),
                                     feature_group_count=SSD_CONV_DIM)
    xbc_c = jax.nn.silu(xbc_c + lp['conv_b'])
    new_conv = xbc_all[:, -(SSD_CONV - 1):]
    xs, Bm, Cm = jnp.split(xbc_c, [SSD_INNER, SSD_INNER + SSD_GROUPS * SSD_STATE], axis=-1)
    dt = jax.nn.softplus(dt.astype(f32) + lp['dt_bias'].astype(f32))
    A = -jnp.exp(lp['a_log'].astype(f32))
    xs = xs.reshape(B, S, SSD_HEADS, SSD_HEAD_DIM)
    y, new_ssm = _ssd(xs, dt, A, Bm.reshape(B, S, SSD_GROUPS, SSD_STATE),
                      Cm.reshape(B, S, SSD_GROUPS, SSD_STATE), ssm0)
    y = y + lp['ssd_d'].astype(f32)[:, None] * xs.astype(f32)
    y = y.reshape(B, S, SSD_INNER) * jax.nn.silu(z.astype(f32))
    y = _rms(y.reshape(B, S, SSD_GROUPS, -1), lp['ssd_norm_g'].reshape(SSD_GROUPS, -1))
    y = y.reshape(B, S, SSD_INNER).astype(x.dtype)

    lam_init = 0.8 - 0.6 * math.exp(-0.3 * layer_idx)
    lq1, lk1, lq2, lk2 = lp['diff_lam'].astype(f32)
    lam = jnp.exp(jnp.sum(lq1 * lk1)) - jnp.exp(jnp.sum(lq2 * lk2)) + lam_init
    q_d = _rope(_rms(dq.reshape(B, S, DIFF_HEADS, 2, DIFF_HEAD_DIM), lp['diff_qn_g']), q_pos)
    k_d = _rope(_rms(dk.reshape(B, S, DIFF_HEADS, 2, DIFF_HEAD_DIM), lp['diff_kn_g']), q_pos)
    v_d = dv.reshape(B, S, DIFF_HEADS, 2 * DIFF_HEAD_DIM)
    k_all = jnp.concatenate([dk_past.reshape(B, P0, DIFF_HEADS, 2, DIFF_HEAD_DIM), k_d], axis=1)
    v_all = jnp.concatenate([dv_past, v_d], axis=1)
    o_d = _diff_attention(q_d, k_all, v_all, lam, q_pos, k_pos)
    o_diff = (_rms(o_d, lp['diff_subln_g']) * (1.0 - lam_init)).reshape(B, S, DIFF_W)

    ga, gb, gc = jnp.split(jax.nn.sigmoid(gates), N_BRANCH, axis=-1)
    m = (ga * (o_fox @ lp['w_br_fox']) + gb * (y @ lp['w_br_ssd'])
         + gc * (o_diff @ lp['w_br_diff']))
    x = x + g1 * (m @ lp['w_out'])

    h2 = _rms(x, lp['norm2_g']) * (1 + sc2) + sh2
    x = x + g2 * _peer(h2, lp['peer_wq'], lp['peer_keys'], lp['peer_u'], lp['peer_v'])

    new_rows = (k_f, v_f, logf, k_d.reshape(B, S, DIFF_HEADS, 2 * DIFF_HEAD_DIM), v_d, new_ssm, new_conv)
    return x, new_rows


def setup_inputs(seed: int = 0) -> dict:
    key = jax.random.key(seed)
    ks = iter(jax.random.split(key, 48))
    f32 = jnp.float32
    D = D_MODEL
    n_pages = PAST_LEN // PAGE_SIZE
    n_used = DEC_BATCH * n_pages
    n_pool = n_used + max(1, n_used // 4)
    pool = (DEPTH, n_pool, PAGE_SIZE)

    def nrm(shape, scale=1.0):
        return scale * jax.random.normal(next(ks), shape, f32)

    def gain(shape):
        return 1.0 + nrm(shape, 0.05)

    page_table = jax.random.permutation(next(ks), n_pool)[:n_used].reshape(DEC_BATCH, n_pages).astype(jnp.int32)
    dt0 = jnp.exp(jax.random.uniform(next(ks), (DEPTH, SSD_HEADS), f32, math.log(1e-3), math.log(1e-1)))
    dt_bias = dt0 + jnp.log(-jnp.expm1(-dt0))
    a_log = jnp.log(jax.random.uniform(next(ks), (DEPTH, SSD_HEADS), f32, 1.0, 16.0))
    return {
        'x_prompt': nrm((BATCH, SEQ, D)),
        'x_sample': nrm((DEC_BATCH, DEC_SEQ, D)),
        'c_prompt': nrm((BATCH, D)),
        'c_sample': nrm((DEC_BATCH, D)),
        'cache_fox_k': nrm(pool + (FOX_HEADS, FOX_HEAD_DIM)),
        'cache_fox_v': nrm(pool + (FOX_HEADS, FOX_HEAD_DIM)),
        'cache_fox_logf': jax.nn.log_sigmoid(3.0 + nrm(pool + (FOX_HEADS,))),
        'cache_diff_k': nrm(pool + (DIFF_HEADS, 2 * DIFF_HEAD_DIM)),
        'cache_diff_v': nrm(pool + (DIFF_HEADS, 2 * DIFF_HEAD_DIM)),
        'state_ssm': nrm((DEPTH, DEC_BATCH, SSD_HEADS, SSD_HEAD_DIM, SSD_STATE), 0.3),
        'state_conv': nrm((DEPTH, DEC_BATCH, SSD_CONV - 1, SSD_CONV_DIM)),
        'page_table': page_table,
        'ada_w': nrm((DEPTH, D, 6 * D), 0.5 * D ** -0.5),
        'ada_b': nrm((DEPTH, 6 * D), 0.02),
        'norm1_g': gain((DEPTH, D)),
        'norm2_g': gain((DEPTH, D)),
        'w_in': nrm((DEPTH, D, IN_COLS), D ** -0.5),
        'fox_f_b': 3.0 + nrm((DEPTH, FOX_HEADS), 0.5),
        'fox_qn_g': gain((DEPTH, FOX_HEAD_DIM)),
        'fox_kn_g': gain((DEPTH, FOX_HEAD_DIM)),
        'conv_w': nrm((DEPTH, SSD_CONV, SSD_CONV_DIM), SSD_CONV ** -0.5),
        'conv_b': nrm((DEPTH, SSD_CONV_DIM), 0.02),
        'dt_bias': dt_bias,
        'a_log': a_log,
        'ssd_d': 1.0 + nrm((DEPTH, SSD_HEADS), 0.1),
        'ssd_norm_g': gain((DEPTH, SSD_INNER)),
        'diff_qn_g': gain((DEPTH, DIFF_HEAD_DIM)),
        'diff_kn_g': gain((DEPTH, DIFF_HEAD_DIM)),
        'diff_lam': nrm((DEPTH, 4, DIFF_HEAD_DIM), 0.1),
        'diff_subln_g': gain((DEPTH, 2 * DIFF_HEAD_DIM)),
        'w_br_fox': nrm((DEPTH, FOX_W, D), FOX_W ** -0.5),
        'w_br_ssd': nrm((DEPTH, SSD_INNER, D), SSD_INNER ** -0.5),
        'w_br_diff': nrm((DEPTH, DIFF_W, D), DIFF_W ** -0.5),
        'w_out': nrm((DEPTH, D, D), D ** -0.5),
        'peer_wq': nrm((DEPTH, D, PEER_HEADS * PEER_QDIM), D ** -0.5),
        'peer_keys': nrm((DEPTH, 2, PEER_KEYS, PEER_QDIM // 2), (PEER_QDIM // 2) ** -0.5),
        'peer_u': nrm((DEPTH, PEER_EXPERTS, D), D ** -0.5),
        'peer_v': nrm((DEPTH, PEER_EXPERTS, D), 0.5),
    }


def reference(x_prompt, x_sample, c_prompt, c_sample, cache_fox_k, cache_fox_v, cache_fox_logf,
              cache_diff_k, cache_diff_v, state_ssm, state_conv, page_table, ada_w, ada_b, norm1_g,
              norm2_g, w_in, fox_f_b, fox_qn_g, fox_kn_g, conv_w, conv_b, dt_bias, a_log, ssd_d,
              ssd_norm_g, diff_qn_g, diff_kn_g, diff_lam, diff_subln_g, w_br_fox, w_br_ssd, w_br_diff,
              w_out, peer_wq, peer_keys, peer_u, peer_v):
    bp, dtp = x_prompt.shape[0], x_prompt.dtype
    xp, xs = x_prompt, x_sample
    rows_p = [[] for _ in range(7)]
    rows_s = [[] for _ in range(7)]
    for l in range(DEPTH):
        lp = {
            'ada_w': ada_w[l], 'ada_b': ada_b[l], 'norm1_g': norm1_g[l], 'norm2_g': norm2_g[l],
            'w_in': w_in[l], 'fox_f_b': fox_f_b[l], 'fox_qn_g': fox_qn_g[l], 'fox_kn_g': fox_kn_g[l],
            'conv_w': conv_w[l], 'conv_b': conv_b[l], 'dt_bias': dt_bias[l], 'a_log': a_log[l],
            'ssd_d': ssd_d[l], 'ssd_norm_g': ssd_norm_g[l], 'diff_qn_g': diff_qn_g[l],
            'diff_kn_g': diff_kn_g[l], 'diff_lam': diff_lam[l], 'diff_subln_g': diff_subln_g[l],
            'w_br_fox': w_br_fox[l], 'w_br_ssd': w_br_ssd[l], 'w_br_diff': w_br_diff[l],
            'w_out': w_out[l], 'peer_wq': peer_wq[l], 'peer_keys': peer_keys[l],
            'peer_u': peer_u[l], 'peer_v': peer_v[l],
        }
        xp, new_p = _layer(
            xp, c_prompt, lp, l,
            jnp.zeros((bp, 0, FOX_HEADS, FOX_HEAD_DIM), dtp),
            jnp.zeros((bp, 0, FOX_HEADS, FOX_HEAD_DIM), dtp),
            jnp.zeros((bp, 0, FOX_HEADS), jnp.float32),
            jnp.zeros((bp, 0, DIFF_HEADS, 2 * DIFF_HEAD_DIM), dtp),
            jnp.zeros((bp, 0, DIFF_HEADS, 2 * DIFF_HEAD_DIM), dtp),
            jnp.zeros((bp, SSD_HEADS, SSD_HEAD_DIM, SSD_STATE), jnp.float32),
            jnp.zeros((bp, SSD_CONV - 1, SSD_CONV_DIM), dtp))
        xs, new_s = _layer(
            xs, c_sample, lp, l,
            _gather_pages(cache_fox_k, l, page_table),
            _gather_pages(cache_fox_v, l, page_table),
            _gather_pages(cache_fox_logf, l, page_table),
            _gather_pages(cache_diff_k, l, page_table),
            _gather_pages(cache_diff_v, l, page_table),
            state_ssm[l], state_conv[l])
        for i in range(7):
            rows_p[i].append(new_p[i])
            rows_s[i].append(new_s[i])
    fox_k_p, fox_v_p, fox_logf_p, diff_k_p, diff_v_p, ssm_p, conv_p = [jnp.stack(r) for r in rows_p]
    fox_k_s, fox_v_s, fox_logf_s, diff_k_s, diff_v_s, ssm_s, conv_s = [jnp.stack(r) for r in rows_s]
    return (xp, xs, fox_k_p, fox_k_s, fox_v_p, fox_v_s, fox_logf_p, fox_logf_s,
            diff_k_p, diff_k_s, diff_v_p, diff_v_s, ssm_p, ssm_s, conv_p, conv_s)
```

```python
import functools
import math

import numpy as np
import jax
import jax.numpy as jnp
from jax import lax
from jax.experimental import pallas as pl
from jax.experimental.pallas import tpu as pltpu

F32 = jnp.float32
BF16 = jnp.bfloat16

D_MODEL = 1024
PAGE = 128
FOX_HEADS = 8
HEAD_DIM = 64
ATT_W = 512
DIFF_HEADS = 4
SSD_INNER = 1024
SSD_HEADS = 16
SSD_HEAD_DIM = 64
SSD_GROUPS = 4
SSD_STATE = 128
SSD_CONV = 4
SSD_CONV_DIM = 2048
SSD_CHUNK = 128
PEER_HEADS = 8
PEER_KEYS = 128
PEER_EXPERTS = PEER_KEYS * PEER_KEYS
PEER_TOPK = 16
PEER_QDIM = 256
ROPE_THETA = 10000.0
EPS = 1e-6
IN_SPLITS = (512, 512, 512, 8, 1024, 2048, 16, 512, 512, 512, 3072)
IN_OFF = tuple(int(v) for v in np.cumsum((0,) + IN_SPLITS))

LANES = 128
SUBLANES = 8
VMEM_LIMIT = 56 * 1024 * 1024

NEG_INF = float("-inf")


def _cparams(*sem):
    return pltpu.CompilerParams(dimension_semantics=sem, vmem_limit_bytes=VMEM_LIMIT)


def _dot(a, b):
    return jnp.dot(a, b, preferred_element_type=F32)


def _dot_nt(a, b):
    return lax.dot_general(a, b, (((1,), (1,)), ((), ())), preferred_element_type=F32)


def _split3(x):
    hi = x.astype(BF16)
    r1 = x - hi.astype(F32)
    mid = r1.astype(BF16)
    lo = (r1 - mid.astype(F32)).astype(BF16)
    return hi, mid, lo


def _dot_sel_r(x, m01):
    hi, mid, lo = _split3(x)
    return _dot(hi, m01) + _dot(mid, m01) + _dot(lo, m01)


def _dot_sel_l(m01, x):
    hi, mid, lo = _split3(x)
    return _dot(m01, hi) + _dot(m01, mid) + _dot(m01, lo)


def _rms_mod(x, g, sc, sh):
    ms = jnp.mean(x * x, axis=-1, keepdims=True)
    return (x * lax.rsqrt(ms + EPS) * g) * (1.0 + sc) + sh


def _silu(x):
    return x * (1.0 / (1.0 + jnp.exp(-x)))


def _sigmoid(x):
    return 1.0 / (1.0 + jnp.exp(-x))


def _log_sigmoid(x):
    return jnp.minimum(x, 0.0) - jnp.log1p(jnp.exp(-jnp.abs(x)))


def _softplus(x):
    return jnp.maximum(x, 0.0) + jnp.log1p(jnp.exp(-jnp.abs(x)))


def _head_norm(a, seg, g):
    ss = _dot_sel_r(a * a, seg)
    return a * lax.rsqrt(ss * (1.0 / HEAD_DIM) + EPS) * g


def _ada_kernel(c_ref, w_ref, b_ref, o_ref):
    c = c_ref[...]
    s = _silu(c)
    w = w_ref[0]
    s_hi = s.astype(BF16)
    s_lo = (s - s_hi.astype(F32)).astype(BF16)
    w_hi = w.astype(BF16)
    w_lo = (w - w_hi.astype(F32)).astype(BF16)
    o_ref[0] = _dot(s_hi, w_hi) + _dot(s_hi, w_lo) + _dot(s_lo, w_hi) + b_ref[0]


def _ada_mod(c_all, ada_w, ada_b):
    depth = ada_w.shape[0]
    r = c_all.shape[0]
    tn = 1024
    return pl.pallas_call(
        _ada_kernel,
        grid=(depth, 6 * D_MODEL // tn),
        in_specs=[
            pl.BlockSpec((r, D_MODEL), lambda l, j: (0, 0)),
            pl.BlockSpec((1, D_MODEL, tn), lambda l, j: (l, 0, j)),
            pl.BlockSpec((1, 1, tn), lambda l, j: (l, 0, j)),
        ],
        out_specs=pl.BlockSpec((1, r, tn), lambda l, j: (l, 0, j)),
        out_shape=jax.ShapeDtypeStruct((depth, r, 6 * D_MODEL), F32),
        compiler_params=_cparams("parallel", "parallel"),
    )(c_all, ada_w, ada_b.reshape(depth, 1, 6 * D_MODEL))


def _mod_specs(tm, rows_per_group, mod_rows):
    tiles_per_group = rows_per_group // tm
    if mod_rows == 1:
        return pl.BlockSpec((1, 1, D_MODEL), lambda i: (i // tiles_per_group, 0, 0))
    return pl.BlockSpec((1, tm, D_MODEL), lambda i: (i // tiles_per_group, i % tiles_per_group, 0))


def _full(shape):
    nd = len(shape)
    return pl.BlockSpec(shape, lambda *_: (0,) * nd)


def _fox_proj_kernel(x_ref, sc_ref, sh_ref, g_ref, w_ref, wft_ref, fb_ref, fbt_ref, qg_ref, kg_ref, seg_ref,
                     q_out, kf_out, kb_out, vf_out, vb_out, lf_out, lft_out):
    h = _rms_mod(x_ref[...], g_ref[...], sc_ref[0], sh_ref[0]).astype(BF16)
    seg = seg_ref[...]
    fq = _dot(h, w_ref[:, 0:ATT_W])
    q_out[...] = (_head_norm(fq, seg, qg_ref[...]) * (HEAD_DIM ** -0.5)).astype(BF16)
    fk = _head_norm(_dot(h, w_ref[:, ATT_W:2 * ATT_W]), seg, kg_ref[...])
    kf_out[...] = fk
    kb_out[...] = fk.astype(BF16)
    fv = _dot(h, w_ref[:, 2 * ATT_W:3 * ATT_W])
    vf_out[...] = fv
    vb_out[...] = fv.astype(BF16)
    ff = _dot(h, w_ref[:, 3 * ATT_W:3 * ATT_W + LANES])
    lf_out[...] = _log_sigmoid(ff[:, 0:FOX_HEADS] + fb_ref[...])
    fft = _dot_nt(wft_ref[...], h)
    lft_out[...] = _log_sigmoid(fft[0:FOX_HEADS, :] + fbt_ref[...])


def _fox_proj(x2, sc, sh, g, w, wft, fb, fbt, qg, kg, seg, *, tm, rows_per_group):
    t = x2.shape[0]
    mod_spec = _mod_specs(tm, rows_per_group, sc.shape[1])
    row = lambda n: pl.BlockSpec((tm, n), lambda i: (i, 0))
    return pl.pallas_call(
        _fox_proj_kernel,
        grid=(t // tm,),
        in_specs=[row(D_MODEL), mod_spec, mod_spec, _full((1, D_MODEL)), _full(w.shape), _full(wft.shape),
                  _full(fb.shape), _full(fbt.shape), _full(qg.shape), _full(kg.shape), _full(seg.shape)],
        out_specs=[row(ATT_W), row(ATT_W), row(ATT_W), row(ATT_W), row(ATT_W), row(FOX_HEADS),
                   pl.BlockSpec((FOX_HEADS, tm), lambda i: (0, i))],
        out_shape=[jax.ShapeDtypeStruct((t, ATT_W), BF16), jax.ShapeDtypeStruct((t, ATT_W), F32),
                   jax.ShapeDtypeStruct((t, ATT_W), BF16), jax.ShapeDtypeStruct((t, ATT_W), F32),
                   jax.ShapeDtypeStruct((t, ATT_W), BF16), jax.ShapeDtypeStruct((t, FOX_HEADS), F32),
                   jax.ShapeDtypeStruct((FOX_HEADS, t), F32)],
        compiler_params=_cparams("parallel"),
    )(x2, sc, sh, g, w, wft, fb, fbt, qg, kg, seg)


def _rope(a, cos, sin_signed, first_half):
    half = HEAD_DIM // 2
    chunks = []
    for c in range(a.shape[1] // LANES):
        ac = a[:, c * LANES:(c + 1) * LANES]
        chunks.append(jnp.where(first_half[:, c * LANES:(c + 1) * LANES],
                                pltpu.roll(ac, LANES - half, 1), pltpu.roll(ac, half, 1)))
    return a * cos + jnp.concatenate(chunks, axis=1) * sin_signed


def _diff_proj_kernel(x_ref, sc_ref, sh_ref, g_ref, w_ref, qg_ref, kg_ref, seg_ref, cos_ref, sin_ref,
                      q_out, kf_out, kb_out, vf_out, vb_out):
    h = _rms_mod(x_ref[...], g_ref[...], sc_ref[0], sh_ref[0]).astype(BF16)
    seg = seg_ref[...]
    cos = jnp.concatenate([cos_ref[...]] * (ATT_W // LANES), axis=1)
    sin = jnp.concatenate([sin_ref[...]] * (ATT_W // LANES), axis=1)
    lane = lax.broadcasted_iota(jnp.int32, (1, ATT_W), 1)
    first_half = (lane % HEAD_DIM) < (HEAD_DIM // 2)
    dq = _head_norm(_dot(h, w_ref[:, 0:ATT_W]), seg, qg_ref[...])
    q_out[...] = (_rope(dq, cos, sin, first_half) * (HEAD_DIM ** -0.5)).astype(BF16)
    dk = _rope(_head_norm(_dot(h, w_ref[:, ATT_W:2 * ATT_W]), seg, kg_ref[...]), cos, sin, first_half)
    kf_out[...] = dk
    kb_out[...] = dk.astype(BF16)
    dv = _dot(h, w_ref[:, 2 * ATT_W:3 * ATT_W])
    vf_out[...] = dv
    vb_out[...] = dv.astype(BF16)


def _diff_proj(x2, sc, sh, g, w, qg, kg, seg, cos, sin, *, tm, rows_per_group):
    t = x2.shape[0]
    mod_spec = _mod_specs(tm, rows_per_group, sc.shape[1])
    row = lambda n: pl.BlockSpec((tm, n), lambda i: (i, 0))
    tiles_per_group = rows_per_group // tm
    pos_spec = pl.BlockSpec((tm, LANES), lambda i: (i % tiles_per_group, 0))
    return pl.pallas_call(
        _diff_proj_kernel,
        grid=(t // tm,),
        in_specs=[row(D_MODEL), mod_spec, mod_spec, _full((1, D_MODEL)), _full(w.shape),
                  _full(qg.shape), _full(kg.shape), _full(seg.shape), pos_spec, pos_spec],
        out_specs=[row(ATT_W)] * 5,
        out_shape=[jax.ShapeDtypeStruct((t, ATT_W), BF16), jax.ShapeDtypeStruct((t, ATT_W), F32),
                   jax.ShapeDtypeStruct((t, ATT_W), BF16), jax.ShapeDtypeStruct((t, ATT_W), F32),
                   jax.ShapeDtypeStruct((t, ATT_W), BF16)],
        compiler_params=_cparams("parallel"),
    )(x2, sc, sh, g, w, qg, kg, seg, cos, sin)


def _ssd_proj_kernel(x_ref, sc_ref, sh_ref, g_ref, w_ref, dtb_ref, z_out, xbc_out, dt_out):
    h = _rms_mod(x_ref[...], g_ref[...], sc_ref[0], sh_ref[0]).astype(BF16)
    z_out[...] = _dot(h, w_ref[:, 0:SSD_INNER])
    for j in range(SSD_CONV_DIM // 512):
        lo = SSD_INNER + j * 512
        xbc_out[:, j * 512:(j + 1) * 512] = _dot(h, w_ref[:, lo:lo + 512])
    lo = SSD_INNER + SSD_CONV_DIM
    dt_out[...] = _softplus(_dot(h, w_ref[:, lo:lo + LANES]) + dtb_ref[...])


def _ssd_proj(x2, sc, sh, g, w, dtb, *, tm, rows_per_group):
    t = x2.shape[0]
    mod_spec = _mod_specs(tm, rows_per_group, sc.shape[1])
    row = lambda n: pl.BlockSpec((tm, n), lambda i: (i, 0))
    return pl.pallas_call(
        _ssd_proj_kernel,
        grid=(t // tm,),
        in_specs=[row(D_MODEL), mod_spec, mod_spec, _full((1, D_MODEL)), _full(w.shape), _full(dtb.shape)],
        out_specs=[row(SSD_INNER), row(SSD_CONV_DIM), row(LANES)],
        out_shape=[jax.ShapeDtypeStruct((t, SSD_INNER), F32), jax.ShapeDtypeStruct((t, SSD_CONV_DIM), F32),
                   jax.ShapeDtypeStruct((t, LANES), F32)],
        compiler_params=_cparams("parallel"),
    )(x2, sc, sh, g, w, dtb)


def _cumsum_kernel(x_ref, tri_ref, o_ref):
    tri = tri_ref[...]
    n = x_ref.shape[1] // LANES
    carry = jnp.zeros((x_ref.shape[0], 1), F32)
    for c in range(n):
        cs = _dot_sel_r(x_ref[:, c * LANES:(c + 1) * LANES], tri) + carry
        o_ref[:, c * LANES:(c + 1) * LANES] = cs
        carry = cs[:, LANES - 1:LANES]


def _cumsum_rows(x, tri, seq):
    r, t = x.shape
    return pl.pallas_call(
        _cumsum_kernel,
        grid=(t // seq,),
        in_specs=[pl.BlockSpec((r, seq), lambda b: (0, b)), _full(tri.shape)],
        out_specs=pl.BlockSpec((r, seq), lambda b: (0, b)),
        out_shape=jax.ShapeDtypeStruct((r, t), F32),
        compiler_params=_cparams("parallel"),
    )(x, tri)


def _diff_lambda(dl, lam_init):
    a = jnp.sum(dl[0:1, :] * dl[1:2, :], axis=1, keepdims=True)
    b = jnp.sum(dl[2:3, :] * dl[3:4, :], axis=1, keepdims=True)
    return jnp.exp(a) - jnp.exp(b) + lam_init


def _attn_kernel(*refs, fox, tq, tk, lam_init):
    if fox:
        q_ref, k_ref, v_ref, f_ref, o_ref, m_sc, l_sc, acc_sc = refs
    else:
        q_ref, k_ref, v_ref, dl_ref, sg_ref, o_ref, m_sc, l_sc, acc_sc = refs
    qi = pl.program_id(2)
    ki = pl.program_id(3)

    @pl.when(ki == 0)
    def _():
        m_sc[...] = jnp.full(m_sc.shape, NEG_INF, F32)
        l_sc[...] = jnp.zeros(l_sc.shape, F32)
        acc_sc[...] = jnp.zeros(acc_sc.shape, F32)

    @pl.when(ki <= qi)
    def _():
        q = q_ref[0]
        k = k_ref[0]
        v = v_ref[0]
        lane = lax.broadcasted_iota(jnp.int32, (1, LANES), 1)
        rows = qi * tq + lax.broadcasted_iota(jnp.int32, (tq, tk), 0)
        cols = ki * tk + lax.broadcasted_iota(jnp.int32, (tq, tk), 1)
        visible = cols <= rows
        for i in range(2):
            in_half = (lane >= i * HEAD_DIM) & (lane < (i + 1) * HEAD_DIM)
            qm = jnp.where(in_half, q, jnp.zeros_like(q))
            s = _dot_nt(qm, k)
            if fox:
                s = s - f_ref[0, i:i + 1, :]
            s = jnp.where(visible, s, NEG_INF)
            m_prev = m_sc[i]
            m_new = jnp.maximum(m_prev, jnp.max(s, axis=1, keepdims=True))
            alpha = jnp.exp(m_prev - m_new)
            p = jnp.exp(s - m_new)
            l_sc[i] = alpha * l_sc[i] + jnp.sum(p, axis=1, keepdims=True)
            acc_sc[i] = alpha * acc_sc[i] + _dot(p.astype(BF16), v)
            m_sc[i] = m_new

    @pl.when(ki == pl.num_programs(3) - 1)
    def _():
        o0 = acc_sc[0] / l_sc[0]
        o1 = acc_sc[1] / l_sc[1]
        if fox:
            lane = lax.broadcasted_iota(jnp.int32, (1, LANES), 1)
            o_ref[0] = jnp.where(lane < HEAD_DIM, o0, o1).astype(BF16)
        else:
            o = o0 - _diff_lambda(dl_ref[...], lam_init) * o1
            ms = jnp.mean(o * o, axis=1, keepdims=True)
            o_ref[0] = (o * lax.rsqrt(ms + EPS) * sg_ref[...] * (1.0 - lam_init)).astype(BF16)


def _prompt_attention(q, k, v, extra, *, fox, lam_init=0.0):
    b, s, _ = q.shape
    tq = tk = min(512, s)
    nq = s // tq
    qspec = pl.BlockSpec((1, tq, LANES), lambda bi, c, qi, ki: (bi, qi, c))
    kspec = pl.BlockSpec((1, tk, LANES), lambda bi, c, qi, ki: (bi, jnp.minimum(ki, qi), c))
    if fox:
        (fneg,) = extra
        espec = [pl.BlockSpec((1, 2, tk), lambda bi, c, qi, ki: (c, 0, bi * nq + jnp.minimum(ki, qi)))]
    else:
        dl, sg = extra
        espec = [_full(dl.shape), _full(sg.shape)]
    return pl.pallas_call(
        functools.partial(_attn_kernel, fox=fox, tq=tq, tk=tk, lam_init=lam_init),
        grid=(b, ATT_W // LANES, nq, nq),
        in_specs=[qspec, kspec, kspec] + espec,
        out_specs=qspec,
        out_shape=jax.ShapeDtypeStruct((b, s, ATT_W), BF16),
        scratch_shapes=[pltpu.VMEM((2, tq, 1), F32), pltpu.VMEM((2, tq, 1), F32),
                        pltpu.VMEM((2, tq, LANES), F32)],
        compiler_params=_cparams("parallel", "parallel", "parallel", "arbitrary"),
    )(q, k, v, *extra)


PAGES_PER_STEP = 4
N_ROWS = 32


def _decode_kernel(*refs, fox, n_q, lam_init):
    pps = PAGES_PER_STEP
    pt_ref = refs[0]
    del pt_ref
    pos = 1
    q_ref = refs[pos]; pos += 1
    k_refs = refs[pos:pos + pps]; pos += pps
    v_refs = refs[pos:pos + pps]; pos += pps
    if fox:
        f_refs = refs[pos:pos + pps]; pos += pps
    kn_ref, vn_ref = refs[pos], refs[pos + 1]; pos += 2
    if fox:
        fn_ref, tri_ref = refs[pos], refs[pos + 1]; pos += 2
    else:
        dl_ref, sg_ref = refs[pos], refs[pos + 1]; pos += 2
    o_ref = refs[pos]; pos += 1
    qbd_sc, m_sc, l_sc, acc_sc, car_sc = refs[pos:pos + 5]
    j = pl.program_id(1)
    n_groups = ATT_W // HEAD_DIM

    @pl.when(j == 0)
    def _():
        grp = lax.broadcasted_iota(jnp.int32, (n_groups, ATT_W), 0)
        col = lax.broadcasted_iota(jnp.int32, (n_groups, ATT_W), 1)
        own = (col // HEAD_DIM) == grp
        for qq in range(n_q):
            row = jnp.broadcast_to(q_ref[0, qq:qq + 1, :], (n_groups, ATT_W))
            qbd_sc[qq * n_groups:(qq + 1) * n_groups, :] = jnp.where(own, row, 0.0).astype(BF16)
        m_sc[...] = jnp.full(m_sc.shape, NEG_INF, F32)
        l_sc[...] = jnp.zeros(l_sc.shape, F32)
        acc_sc[...] = jnp.zeros(acc_sc.shape, F32)
        car_sc[...] = jnp.zeros(car_sc.shape, F32)

    def page(k_page, v_page, f_page, visible):
        s = _dot_nt(qbd_sc[...], k_page.astype(BF16))
        if fox:
            cs = _dot_sel_r(f_page, tri_ref[...]) + car_sc[...]
            car_sc[...] = cs[:, PAGE - 1:PAGE]
            s = s - jnp.concatenate([cs] * n_q, axis=0)
        if visible is not None:
            s = jnp.where(visible, s, NEG_INF)
        m_prev = m_sc[...]
        m_new = jnp.maximum(m_prev, jnp.max(s, axis=1, keepdims=True))
        alpha = jnp.exp(m_prev - m_new)
        p = jnp.exp(s - m_new)
        l_sc[...] = alpha * l_sc[...] + jnp.sum(p, axis=1, keepdims=True)
        acc_sc[...] = alpha * acc_sc[...] + _dot(p.astype(BF16), v_page.astype(BF16))
        m_sc[...] = m_new

    for u in range(pps):
        page(k_refs[u][...], v_refs[u][...], f_refs[u][...] if fox else None, None)

    @pl.when(j == pl.num_programs(1) - 1)
    def _():
        key = lax.broadcasted_iota(jnp.int32, (N_ROWS, PAGE), 1)
        qrow = lax.broadcasted_iota(jnp.int32, (N_ROWS, PAGE), 0) // n_groups
        page(kn_ref[0], vn_ref[0], fn_ref[0] if fox else None, (key <= qrow) & (key < n_q))
        accn = acc_sc[...] / l_sc[...]
        grp = lax.broadcasted_iota(jnp.int32, (n_groups, ATT_W), 0)
        col = lax.broadcasted_iota(jnp.int32, (n_groups, ATT_W), 1)
        if fox:
            comb = jnp.where((col // HEAD_DIM) == grp, 1.0, 0.0)
        else:
            lam = _diff_lambda(dl_ref[...], lam_init)
            coef = jnp.where(grp % 2 == 0, 1.0, -lam)
            comb = jnp.where((col // (2 * HEAD_DIM)) == (grp // 2), coef, 0.0)
        outs = [jnp.sum(accn[qq * n_groups:(qq + 1) * n_groups, :] * comb, axis=0, keepdims=True)
                for qq in range(n_q)]
        o = jnp.concatenate(outs, axis=0)
        if not fox:
            parts = []
            for hh in range(DIFF_HEADS):
                oh = o[:, hh * LANES:(hh + 1) * LANES]
                ms = jnp.mean(oh * oh, axis=1, keepdims=True)
                parts.append(oh * lax.rsqrt(ms + EPS) * sg_ref[...] * (1.0 - lam_init))
            o = jnp.concatenate(parts, axis=1)
        o_ref[0] = o


def _decode_attention(page_table, layer, q, k_pool, v_pool, f_pool, k_new, v_new, f_new, extra, *, fox,
                      lam_init=0.0):
    db, n_q, _ = q.shape
    n_pages = page_table.shape[1]
    pps = PAGES_PER_STEP
    assert n_pages % pps == 0 and n_q * (ATT_W // HEAD_DIM) == N_ROWS

    def pool_spec(u, rows, cols):
        return pl.BlockSpec((None, None, rows, cols), lambda b, j, pt: (layer, pt[b, j * pps + u], 0, 0))

    per_seq = lambda rows, cols: pl.BlockSpec((1, rows, cols), lambda b, j, pt: (b, 0, 0))
    in_specs = [per_seq(n_q, ATT_W)]
    args = [q]
    in_specs += [pool_spec(u, PAGE, ATT_W) for u in range(pps)]
    args += [k_pool] * pps
    in_specs += [pool_spec(u, PAGE, ATT_W) for u in range(pps)]
    args += [v_pool] * pps
    if fox:
        in_specs += [pool_spec(u, FOX_HEADS, PAGE) for u in range(pps)]
        args += [f_pool] * pps
    in_specs += [per_seq(PAGE, ATT_W), per_seq(PAGE, ATT_W)]
    args += [k_new, v_new]
    if fox:
        (tri,) = extra
        in_specs += [per_seq(FOX_HEADS, PAGE), pl.BlockSpec(tri.shape, lambda b, j, pt: (0, 0))]
        args += [f_new, tri]
    else:
        dl, sg = extra
        in_specs += [pl.BlockSpec(dl.shape, lambda b, j, pt: (0, 0)), pl.BlockSpec(sg.shape, lambda b, j, pt: (0, 0))]
        args += [dl, sg]
    grid_spec = pltpu.PrefetchScalarGridSpec(
        num_scalar_prefetch=1,
        grid=(db, n_pages // pps),
        in_specs=in_specs,
        out_specs=per_seq(n_q, ATT_W),
        scratch_shapes=[pltpu.VMEM((N_ROWS, ATT_W), BF16), pltpu.VMEM((N_ROWS, 1), F32),
                        pltpu.VMEM((N_ROWS, 1), F32), pltpu.VMEM((N_ROWS, ATT_W), F32),
                        pltpu.VMEM((FOX_HEADS, 1), F32)],
    )
    return pl.pallas_call(
        functools.partial(_decode_kernel, fox=fox, n_q=n_q, lam_init=lam_init),
        grid_spec=grid_spec,
        out_shape=jax.ShapeDtypeStruct((db, n_q, ATT_W), F32),
        compiler_params=_cparams("parallel", "arbitrary"),
    )(page_table, *args)


def _ssd_kernel(*refs, has_init):
    if has_init:
        (xbc_ref, z_ref, dt_ref, conv0_ref, s0_ref, cw_ref, cb_ref, alog_ref, dexp_ref, ng_ref, exp_ref, tri_ref,
         y_ref, s_out, xbuf, st_sc) = refs
    else:
        (xbc_ref, z_ref, dt_ref, conv0_ref, cw_ref, cb_ref, alog_ref, dexp_ref, ng_ref, exp_ref, tri_ref,
         y_ref, s_out, xbuf, st_sc) = refs
    L = SSD_CHUNK
    c = pl.program_id(1)

    @pl.when(c == 0)
    def _():
        xbuf[0:SUBLANES, :] = conv0_ref[0]
        if has_init:
            st_sc[...] = jnp.transpose(s0_ref[0].reshape(SSD_INNER, SSD_STATE))
        else:
            st_sc[...] = jnp.zeros(st_sc.shape, F32)

    xbuf[SUBLANES:SUBLANES + L, :] = xbc_ref[0]
    conv = cb_ref[...] + cw_ref[3:4, :] * xbuf[SUBLANES:SUBLANES + L, :]
    for kk in range(1, SSD_CONV):
        conv = conv + cw_ref[3 - kk:4 - kk, :] * xbuf[SUBLANES - kk:SUBLANES - kk + L, :]
    xbuf[0:SUBLANES, :] = xbuf[L:L + SUBLANES, :]
    xc = _silu(conv)
    xs = xc[:, 0:SSD_INNER]
    gw = SSD_GROUPS * SSD_STATE
    bm = xc[:, SSD_INNER:SSD_INNER + gw]
    cm = xc[:, SSD_INNER + gw:SSD_INNER + 2 * gw]

    dt = dt_ref[0]
    a = dt * (-jnp.exp(alog_ref[...]))
    acum = _dot_sel_l(tri_ref[...], a)
    acum_t = jnp.transpose(acum)
    total = acum[L - 1:L, :]
    expand = exp_ref[...]
    dt_e = _dot_sel_r(dt, expand)
    eacum_e = _dot_sel_r(jnp.exp(acum), expand)
    toend_e = _dot_sel_r(jnp.exp(total - acum), expand)
    etotal_e = _dot_sel_r(jnp.exp(total), expand)
    xd = xs * dt_e
    xw = (xd * toend_e).astype(BF16)
    xd_b = xd.astype(BF16)

    ii = lax.broadcasted_iota(jnp.int32, (L, L), 0)
    jj = lax.broadcasted_iota(jnp.int32, (L, L), 1)
    lower = jj <= ii
    lane = lax.broadcasted_iota(jnp.int32, (1, LANES), 1)
    heads_per_group = SSD_HEADS // SSD_GROUPS
    gcols = heads_per_group * SSD_HEAD_DIM
    y_parts = []
    for g in range(SSD_GROUPS):
        bg = bm[:, g * SSD_STATE:(g + 1) * SSD_STATE]
        cg = cm[:, g * SSD_STATE:(g + 1) * SSD_STATE].astype(BF16)
        bg_t = jnp.transpose(bg).astype(BF16)
        cb = _dot(cg, bg_t)
        st_g = st_sc[:, g * gcols:(g + 1) * gcols]
        y_off = _dot(cg, st_g.astype(BF16)) * eacum_e[:, g * gcols:(g + 1) * gcols]
        st_sc[:, g * gcols:(g + 1) * gcols] = (st_g * etotal_e[:, g * gcols:(g + 1) * gcols]
                                               + _dot(bg_t, xw[:, g * gcols:(g + 1) * gcols]))
        for pr in range(heads_per_group // 2):
            h0 = g * heads_per_group + 2 * pr
            mats = []
            for hh in (h0, h0 + 1):
                seg = acum[:, hh:hh + 1] - acum_t[hh:hh + 1, :]
                mats.append((cb * jnp.exp(jnp.where(lower, seg, NEG_INF))).astype(BF16))
            xp = xd_b[:, h0 * SSD_HEAD_DIM:(h0 + 2) * SSD_HEAD_DIM]
            zero = jnp.zeros_like(xp)
            stacked = jnp.concatenate([jnp.where(lane < SSD_HEAD_DIM, xp, zero),
                                       jnp.where(lane >= SSD_HEAD_DIM, xp, zero)], axis=0)
            y_diag = _dot(jnp.concatenate(mats, axis=1), stacked)
            y_parts.append(y_diag + y_off[:, 2 * pr * SSD_HEAD_DIM:(2 * pr + 2) * SSD_HEAD_DIM])
    y = jnp.concatenate(y_parts, axis=1) + dexp_ref[...] * xs
    y = y * _silu(z_ref[0])
    outs = []
    for g in range(SSD_GROUPS):
        yg = y[:, g * gcols:(g + 1) * gcols]
        ms = jnp.mean(yg * yg, axis=1, keepdims=True)
        outs.append(yg * lax.rsqrt(ms + EPS) * ng_ref[:, g * gcols:(g + 1) * gcols])
    y_ref[0] = jnp.concatenate(outs, axis=1).astype(BF16)

    @pl.when(c == pl.num_programs(1) - 1)
    def _():
        s_out[0] = jnp.transpose(st_sc[...]).reshape(SSD_HEADS, SSD_HEAD_DIM, SSD_STATE)


def _ssd(xbc, z, dt, conv0, s0, cw, cb, alog, dexp, ng, expand, tri):
    b, s, _ = xbc.shape
    L = SSD_CHUNK
    has_init = s0 is not None
    blk = lambda n: pl.BlockSpec((1, L, n), lambda bi, c: (bi, c, 0))
    per_b = lambda *shape: pl.BlockSpec((1,) + shape, lambda bi, c: (bi,) + (0,) * len(shape))
    in_specs = [blk(SSD_CONV_DIM), blk(SSD_INNER), blk(LANES), per_b(SUBLANES, SSD_CONV_DIM)]
    args = [xbc, z, dt, conv0]
    if has_init:
        in_specs.append(per_b(SSD_HEADS, SSD_HEAD_DIM, SSD_STATE))
        args.append(s0)
    consts = [cw, cb, alog, dexp, ng, expand, tri]
    in_specs += [_full(a.shape) for a in consts]
    args += consts
    return pl.pallas_call(
        functools.partial(_ssd_kernel, has_init=has_init),
        grid=(b, s // L),
        in_specs=in_specs,
        out_specs=[blk(SSD_INNER), per_b(SSD_HEADS, SSD_HEAD_DIM, SSD_STATE)],
        out_shape=[jax.ShapeDtypeStruct((b, s, SSD_INNER), BF16),
                   jax.ShapeDtypeStruct((b, SSD_HEADS, SSD_HEAD_DIM, SSD_STATE), F32)],
        scratch_shapes=[pltpu.VMEM((L + SUBLANES, SSD_CONV_DIM), F32), pltpu.VMEM((SSD_STATE, SSD_INNER), F32)],
        compiler_params=_cparams("parallel", "arbitrary"),
    )(*args)


def _merge_kernel(x_ref, sc_ref, sh_ref, g1_ref, ng_ref, a_ref, b_ref, d_ref, wg_ref, wa_ref, wb_ref, wd_ref,
                  wo_ref, o_ref):
    x = x_ref[...]
    h = _rms_mod(x, ng_ref[...], sc_ref[0], sh_ref[0]).astype(BF16)
    m = _sigmoid(_dot(h, wg_ref[:, 0:D_MODEL])) * _dot(a_ref[...], wa_ref[...])
    m = m + _sigmoid(_dot(h, wg_ref[:, D_MODEL:2 * D_MODEL])) * _dot(b_ref[...], wb_ref[...])
    m = m + _sigmoid(_dot(h, wg_ref[:, 2 * D_MODEL:3 * D_MODEL])) * _dot(d_ref[...], wd_ref[...])
    o_ref[...] = x + g1_ref[0] * _dot(m.astype(BF16), wo_ref[...])


def _merge(x2, sc, sh, g1, ng, o_fox, y_ssd, o_diff, wg, wa, wb, wd, wo, *, tm, rows_per_group):
    t = x2.shape[0]
    mod_spec = _mod_specs(tm, rows_per_group, sc.shape[1])
    row = lambda n: pl.BlockSpec((tm, n), lambda i: (i, 0))
    return pl.pallas_call(
        _merge_kernel,
        grid=(t // tm,),
        in_specs=[row(D_MODEL), mod_spec, mod_spec, mod_spec, _full((1, D_MODEL)), row(ATT_W), row(SSD_INNER),
                  row(ATT_W), _full(wg.shape), _full(wa.shape), _full(wb.shape), _full(wd.shape), _full(wo.shape)],
        out_specs=row(D_MODEL),
        out_shape=jax.ShapeDtypeStruct((t, D_MODEL), F32),
        compiler_params=_cparams("parallel"),
    )(x2, sc, sh, g1, ng, o_fox, y_ssd, o_diff, wg, wa, wb, wd, wo)


def _top16(x):
    r = x.shape[0]
    iota = lax.broadcasted_iota(jnp.int32, x.shape, 0)
    vals, idxs = [], []
    for _ in range(PEER_TOPK):
        m = jnp.max(x, axis=0, keepdims=True)
        idx = jnp.min(jnp.where(x == m, iota, r), axis=0, keepdims=True)
        vals.append(m)
        idxs.append(idx)
        x = jnp.where(iota == idx, NEG_INF, x)
    return jnp.concatenate(vals, axis=0), jnp.concatenate(idxs, axis=0)


def _pick_rows(table, sel):
    out = jnp.zeros(sel.shape, jnp.int32)
    for kk in range(PEER_TOPK):
        out = jnp.where(sel == kk, table[kk:kk + 1, :], out)
    return out


def _route_kernel(x_ref, sc_ref, sh_ref, ng_ref, wq_ref, keys_ref, h_out, g_out, i1_sc, i2_sc, gw_sc, *, tm):
    h = _rms_mod(x_ref[...], ng_ref[...], sc_ref[0], sh_ref[0]).astype(BF16)
    h_out[...] = h
    i1_rows, i2_rows, gw_rows = [], [], []
    for hd in range(PEER_HEADS):
        tops = []
        for i in range(2):
            lo = (hd * 2 + i) * PEER_KEYS
            qh = _dot(h, wq_ref[:, lo:lo + PEER_KEYS]).astype(BF16)
            tops.append(_top16(_dot_nt(keys_ref[i], qh)))
        (v1, x1), (v2, x2) = tops
        cand = jnp.concatenate([v1[kk:kk + 1, :] + v2 for kk in range(PEER_TOPK)], axis=0)
        best_v, best_c = _top16(cand)
        i1_rows.append(_pick_rows(x1, lax.shift_right_logical(best_c, 4)))
        i2_rows.append(_pick_rows(x2, lax.bitwise_and(best_c, PEER_TOPK - 1)))
        e = jnp.exp(best_v - best_v[0:1, :])
        gw_rows.append(e / jnp.sum(e, axis=0, keepdims=True))
    i1_sc[...] = jnp.transpose(jnp.concatenate(i1_rows, axis=0).astype(F32))
    i2_sc[...] = jnp.transpose(jnp.concatenate(i2_rows, axis=0).astype(F32))
    gw_sc[...] = jnp.transpose(jnp.concatenate(gw_rows, axis=0))

    key_id = lax.broadcasted_iota(jnp.int32, (PEER_KEYS, LANES), 0).astype(F32)

    def per_token(t, carry):
        r1 = i1_sc[pl.ds(t, 1), :]
        r2 = i2_sc[pl.ds(t, 1), :]
        rg = gw_sc[pl.ds(t, 1), :]
        a = jnp.where(key_id == r1, rg, 0.0).astype(BF16)
        bsel = jnp.where(key_id == r2, 1.0, 0.0).astype(BF16)
        g_out[t] = _dot_nt(a, bsel).astype(BF16)
        return carry

    lax.fori_loop(0, tm, per_token, 0)


def _route(x2, sc, sh, ng, wq, keys, *, tm, rows_per_group):
    t = x2.shape[0]
    mod_spec = _mod_specs(tm, rows_per_group, sc.shape[1])
    row = lambda n: pl.BlockSpec((tm, n), lambda i: (i, 0))
    return pl.pallas_call(
        functools.partial(_route_kernel, tm=tm),
        grid=(t // tm,),
        in_specs=[row(D_MODEL), mod_spec, mod_spec, _full((1, D_MODEL)), _full(wq.shape), _full(keys.shape)],
        out_specs=[row(D_MODEL), pl.BlockSpec((tm, PEER_KEYS, PEER_KEYS), lambda i: (i, 0, 0))],
        out_shape=[jax.ShapeDtypeStruct((t, D_MODEL), BF16),
                   jax.ShapeDtypeStruct((t, PEER_KEYS, PEER_KEYS), BF16)],
        scratch_shapes=[pltpu.VMEM((tm, LANES), F32)] * 3,
        compiler_params=_cparams("parallel"),
    )(x2, sc, sh, ng, wq, keys)


def _gelu(a):
    return 0.5 * a * (1.0 + lax.erf(a * (2.0 ** -0.5)))


def _expert_kernel(h_ref, g_ref, u_ref, v_ref, x_ref, g2_ref, o_ref, acc_sc):
    j = pl.program_id(1)

    @pl.when(j == 0)
    def _():
        acc_sc[...] = jnp.zeros(acc_sc.shape, F32)

    a = _dot_nt(h_ref[...], u_ref[...])
    w = (g_ref[...].astype(F32) * _gelu(a)).astype(BF16)
    acc_sc[...] += _dot(w, v_ref[...])

    @pl.when(j == pl.num_programs(1) - 1)
    def _():
        o_ref[...] = x_ref[...] + g2_ref[0] * acc_sc[...]


def _experts(h2, gates, u, v, x2, g2, *, tm, rows_per_group, ec=1024):
    t = h2.shape[0]
    tiles_per_group = rows_per_group // tm
    if g2.shape[1] == 1:
        mod_spec = pl.BlockSpec((1, 1, D_MODEL), lambda i, j: (i // tiles_per_group, 0, 0))
    else:
        mod_spec = pl.BlockSpec((1, tm, D_MODEL), lambda i, j: (i // tiles_per_group, i % tiles_per_group, 0))
    row = pl.BlockSpec((tm, D_MODEL), lambda i, j: (i, 0))
    return pl.pallas_call(
        _expert_kernel,
        grid=(t // tm, PEER_EXPERTS // ec),
        in_specs=[row, pl.BlockSpec((tm, ec), lambda i, j: (i, j)), pl.BlockSpec((ec, D_MODEL), lambda i, j: (j, 0)),
                  pl.BlockSpec((ec, D_MODEL), lambda i, j: (j, 0)), row, mod_spec],
        out_specs=row,
        out_shape=jax.ShapeDtypeStruct((t, D_MODEL), F32),
        scratch_shapes=[pltpu.VMEM((tm, D_MODEL), F32)],
        compiler_params=_cparams("parallel", "arbitrary"),
    )(h2, gates, u, v, x2, g2)


def _consts():
    r = np.arange(ATT_W)
    seg = (r[:, None] // HEAD_DIM == r[None, :] // HEAD_DIM).astype(np.float32)
    i = np.arange(LANES)
    tri_incl = (i[:, None] <= i[None, :]).astype(np.float32)
    tri_time = (i[None, :] <= i[:, None]).astype(np.float32)
    expand = (np.arange(SSD_INNER)[None, :] // SSD_HEAD_DIM == i[:, None]).astype(np.float32)
    return (jnp.asarray(seg, BF16), jnp.asarray(tri_incl, BF16), jnp.asarray(tri_time, BF16),
            jnp.asarray(expand, BF16))


def _rope_tables(positions):
    half = HEAD_DIM // 2
    inv = ROPE_THETA ** (-jnp.arange(half, dtype=F32) / half)
    ang = positions.astype(F32)[:, None] * inv[None, :]
    cos, sin = jnp.cos(ang), jnp.sin(ang)
    cos64 = jnp.concatenate([cos, cos], axis=1)
    sin64 = jnp.concatenate([-sin, sin], axis=1)
    return jnp.concatenate([cos64, cos64], axis=1), jnp.concatenate([sin64, sin64], axis=1)


def _pad_cols(a, n):
    return jnp.pad(a, ((0, 0), (0, n - a.shape[1])))


def _layer_weights(p, l):
    w_in = p['w_in'][l]
    o = IN_OFF
    col = lambda i: w_in[:, o[i]:o[i + 1]]
    w_fox = jnp.concatenate([col(0), col(1), col(2), _pad_cols(col(3), LANES)], axis=1).astype(BF16)
    wft = jnp.pad(col(3).T, ((0, 16 - FOX_HEADS), (0, 0))).astype(BF16)
    w_ssd = jnp.concatenate([col(4), col(5), _pad_cols(col(6), LANES)], axis=1).astype(BF16)
    w_diff = jnp.concatenate([col(7), col(8), col(9)], axis=1).astype(BF16)
    tile8 = lambda g: jnp.tile(g, ATT_W // HEAD_DIM)[None, :]
    return dict(
        w_fox=w_fox, wft=wft, w_ssd=w_ssd, w_diff=w_diff, w_gate=col(10).astype(BF16),
        fb=p['fox_f_b'][l][None, :], fbt=p['fox_f_b'][l][:, None],
        fox_qg=tile8(p['fox_qn_g'][l]), fox_kg=tile8(p['fox_kn_g'][l]),
        diff_qg=tile8(p['diff_qn_g'][l]), diff_kg=tile8(p['diff_kn_g'][l]),
        dtb=_pad_cols(p['dt_bias'][l][None, :], LANES),
        alog=jnp.pad(p['a_log'][l][None, :], ((0, 0), (0, LANES - SSD_HEADS)), constant_values=NEG_INF),
        dexp=jnp.repeat(p['ssd_d'][l], SSD_HEAD_DIM)[None, :],
        ssd_ng=p['ssd_norm_g'][l][None, :],
        conv_w=p['conv_w'][l], conv_b=p['conv_b'][l][None, :],
        dl=p['diff_lam'][l], sg=p['diff_subln_g'][l][None, :],
        norm1=p['norm1_g'][l][None, :], norm2=p['norm2_g'][l][None, :],
        wa=p['w_br_fox'][l].astype(BF16), wb=p['w_br_ssd'][l].astype(BF16), wd=p['w_br_diff'][l].astype(BF16),
        wo=p['w_out'][l].astype(BF16), wq=p['peer_wq'][l].astype(BF16), keys=p['peer_keys'][l].astype(BF16),
        u=p['peer_u'][l].astype(BF16), v=p['peer_v'][l].astype(BF16),
    )


def _layer(x, mods, w, consts, layer_idx, past):
    seg, tri_incl, tri_time, expand = consts
    b, s, _ = x.shape
    t = b * s
    sh1, sc1, g1, sh2, sc2, g2 = mods
    x2 = x.reshape(t, D_MODEL)
    if past is None:
        rows_per_group = s
        tm = min(512, s)
        p0 = 0
    else:
        rows_per_group = t
        tm = t
        p0 = past['page_table'].shape[1] * PAGE
    kw = dict(tm=tm, rows_per_group=rows_per_group)
    lam_init = 0.8 - 0.6 * math.exp(-0.3 * layer_idx)

    q_f, kf32, kfb, vf32, vfb, logf, logf_t = _fox_proj(
        x2, sc1, sh1, w['norm1'], w['w_fox'], w['wft'], w['fb'], w['fbt'], w['fox_qg'], w['fox_kg'], seg, **kw)
    z, xbc, dt = _ssd_proj(x2, sc1, sh1, w['norm1'], w['w_ssd'], w['dtb'], **kw)
    pos = p0 + jnp.arange(s, dtype=jnp.int32)
    if past is not None:
        pos = jnp.tile(pos, b)
    cos, sin = _rope_tables(pos)
    q_d, kd32, kdb, vd32, vdb = _diff_proj(
        x2, sc1, sh1, w['norm1'], w['w_diff'], w['diff_qg'], w['diff_kg'], seg, cos, sin, **kw)

    if past is None:
        fcum = _cumsum_rows(logf_t, tri_incl, s).reshape(FOX_HEADS // 2, 2, t)
        o_fox = _prompt_attention(q_f.reshape(b, s, ATT_W), kfb.reshape(b, s, ATT_W), vfb.reshape(b, s, ATT_W),
                                  (fcum,), fox=True)
        o_diff = _prompt_attention(q_d.reshape(b, s, ATT_W), kdb.reshape(b, s, ATT_W), vdb.reshape(b, s, ATT_W),
                                   (w['dl'], w['sg']), fox=False, lam_init=lam_init)
        y, new_ssm = _ssd(xbc.reshape(b, s, -1), z.reshape(b, s, -1), dt.reshape(b, s, -1),
                          jnp.zeros((b, SUBLANES, SSD_CONV_DIM), F32), None,
                          w['conv_w'], w['conv_b'], w['alog'], w['dexp'], w['ssd_ng'], expand, tri_time)
        new_conv = xbc.reshape(b, s, -1)[:, s - (SSD_CONV - 1):, :]
    else:
        pad_rows = lambda a: jnp.pad(a.reshape(b, s, -1), ((0, 0), (0, PAGE - s), (0, 0)))
        f_new = jnp.pad(logf_t.reshape(FOX_HEADS, b, s).transpose(1, 0, 2), ((0, 0), (0, 0), (0, PAGE - s)))
        o_fox = _decode_attention(past['page_table'], layer_idx, q_f.astype(F32).reshape(b, s, ATT_W),
                                  past['fox_k'], past['fox_v'], past['fox_f'], pad_rows(kf32), pad_rows(vf32),
                                  f_new, (tri_incl,), fox=True)
        o_diff = _decode_attention(past['page_table'], layer_idx, q_d.astype(F32).reshape(b, s, ATT_W),
                                   past['diff_k'], past['diff_v'], None, pad_rows(kd32), pad_rows(vd32),
                                   None, (w['dl'], w['sg']), fox=False, lam_init=lam_init)
        o_fox = o_fox.astype(BF16)
        o_diff = o_diff.astype(BF16)
        conv0 = jnp.pad(past['conv'][layer_idx], ((0, 0), (SUBLANES - (SSD_CONV - 1), 0), (0, 0)))
        y, new_ssm = _ssd(pad_rows(xbc), pad_rows(z), pad_rows(dt), conv0, past['ssm'][layer_idx],
                          w['conv_w'], w['conv_b'], w['alog'], w['dexp'], w['ssd_ng'], expand, tri_time)
        y = y[:, :s, :]
        xbc_all = jnp.concatenate([past['conv'][layer_idx], xbc.reshape(b, s, -1)], axis=1)
        new_conv = xbc_all[:, -(SSD_CONV - 1):, :]

    x1 = _merge(x2, sc1, sh1, g1, w['norm1'], o_fox.reshape(t, ATT_W), y.reshape(t, SSD_INNER),
                o_diff.reshape(t, ATT_W), w['w_gate'], w['wa'], w['wb'], w['wd'], w['wo'], **kw)

    tm_r = min(256, tm)
    h2, gates = _route(x1, sc2, sh2, w['norm2'], w['wq'], w['keys'], tm=tm_r, rows_per_group=rows_per_group)
    x_out = _experts(h2, gates.reshape(t, PEER_EXPERTS), w['u'], w['v'], x1, g2, **kw)

    rows = (kf32.reshape(b, s, FOX_HEADS, HEAD_DIM), vf32.reshape(b, s, FOX_HEADS, HEAD_DIM),
            logf.reshape(b, s, FOX_HEADS), kd32.reshape(b, s, DIFF_HEADS, 2 * HEAD_DIM),
            vd32.reshape(b, s, DIFF_HEADS, 2 * HEAD_DIM), new_ssm, new_conv)
    return x_out.reshape(b, s, D_MODEL), rows


def kernel(x_prompt, x_sample, c_prompt, c_sample, cache_fox_k, cache_fox_v, cache_fox_logf, cache_diff_k,
           cache_diff_v, state_ssm, state_conv, page_table, ada_w, ada_b, norm1_g, norm2_g, w_in, fox_f_b,
           fox_qn_g, fox_kn_g, conv_w, conv_b, dt_bias, a_log, ssd_d, ssd_norm_g, diff_qn_g, diff_kn_g,
           diff_lam, diff_subln_g, w_br_fox, w_br_ssd, w_br_diff, w_out, peer_wq, peer_keys, peer_u, peer_v):
    params = dict(w_in=w_in, fox_f_b=fox_f_b, fox_qn_g=fox_qn_g, fox_kn_g=fox_kn_g, conv_w=conv_w, conv_b=conv_b,
                  dt_bias=dt_bias, a_log=a_log, ssd_d=ssd_d, ssd_norm_g=ssd_norm_g, diff_qn_g=diff_qn_g,
                  diff_kn_g=diff_kn_g, diff_lam=diff_lam, diff_subln_g=diff_subln_g, w_br_fox=w_br_fox,
                  w_br_ssd=w_br_ssd, w_br_diff=w_br_diff, w_out=w_out, peer_wq=peer_wq, peer_keys=peer_keys,
                  peer_u=peer_u, peer_v=peer_v, norm1_g=norm1_g, norm2_g=norm2_g)
    depth = w_in.shape[0]
    bp, sp, _ = x_prompt.shape
    bs, ss, _ = x_sample.shape
    consts = _consts()

    n_c = bp + bs
    c_all = jnp.pad(jnp.concatenate([c_prompt, c_sample], axis=0), ((0, (-n_c) % SUBLANES), (0, 0)))
    mod = _ada_mod(c_all, ada_w, ada_b)

    n_pool = cache_fox_k.shape[1]
    past = dict(
        page_table=page_table,
        fox_k=cache_fox_k.reshape(depth, n_pool, PAGE, ATT_W), fox_v=cache_fox_v.reshape(depth, n_pool, PAGE, ATT_W),
        fox_f=jnp.swapaxes(cache_fox_logf, 2, 3),
        diff_k=cache_diff_k.reshape(depth, n_pool, PAGE, ATT_W), diff_v=cache_diff_v.reshape(depth, n_pool, PAGE, ATT_W),
        ssm=state_ssm, conv=state_conv)

    xp, xs = x_prompt, x_sample
    rows_p, rows_s = [], []
    for l in range(depth):
        w = _layer_weights(params, l)
        mp = [mod[l, :bp, i * D_MODEL:(i + 1) * D_MODEL][:, None, :] for i in range(6)]
        ms = [jnp.repeat(mod[l, bp:n_c, i * D_MODEL:(i + 1) * D_MODEL], ss, axis=0)[None] for i in range(6)]
        xp, rp = _layer(xp, mp, w, consts, l, None)
        xs, rs = _layer(xs, ms, w, consts, l, past)
        rows_p.append(rp)
        rows_s.append(rs)
    stk = lambda rows, i: jnp.stack([r[i] for r in rows])
    return (xp, xs, stk(rows_p, 0), stk(rows_s, 0), stk(rows_p, 1), stk(rows_s, 1), stk(rows_p, 2), stk(rows_s, 2),
            stk(rows_p, 3), stk(rows_s, 3), stk(rows_p, 4), stk(rows_s, 4), stk(rows_p, 5), stk(rows_s, 5),
            stk(rows_p, 6), stk(rows_s, 6))
```

```python
import functools
import math

import numpy as np
import jax
import jax.numpy as jnp
from jax import lax
from jax.experimental import pallas as pl
from jax.experimental.pallas import tpu as pltpu

F32 = jnp.float32
BF16 = jnp.bfloat16

D_MODEL = 1024
PAGE = 128
FOX_HEADS = 8
HEAD_DIM = 64
ATT_W = 512
DIFF_HEADS = 4
SSD_INNER = 1024
SSD_HEADS = 16
SSD_HEAD_DIM = 64
SSD_GROUPS = 4
SSD_STATE = 128
SSD_CONV = 4
SSD_CONV_DIM = 2048
SSD_CHUNK = 128
PEER_HEADS = 8
PEER_KEYS = 128
PEER_EXPERTS = PEER_KEYS * PEER_KEYS
PEER_TOPK = 16
PEER_QDIM = 256
ROPE_THETA = 10000.0
EPS = 1e-6
IN_SPLITS = (512, 512, 512, 8, 1024, 2048, 16, 512, 512, 512, 3072)
IN_OFF = tuple(int(v) for v in np.cumsum((0,) + IN_SPLITS))

LANES = 128
SUBLANES = 8
VMEM_LIMIT = 56 * 1024 * 1024

NEG_INF = float("-inf")
LOG2E = math.log2(math.e)
Q_SCALE = HEAD_DIM ** -0.5 * LOG2E


def _cparams(*sem):
    return pltpu.CompilerParams(dimension_semantics=sem, vmem_limit_bytes=VMEM_LIMIT)


def _dot(a, b):
    return jnp.dot(a, b, preferred_element_type=F32)


def _dot_nt(a, b):
    return lax.dot_general(a, b, (((1,), (1,)), ((), ())), preferred_element_type=F32)


def _split3(x):
    hi = x.astype(BF16)
    r1 = x - hi.astype(F32)
    mid = r1.astype(BF16)
    lo = (r1 - mid.astype(F32)).astype(BF16)
    return hi, mid, lo


def _dot_sel_r(x, m01):
    hi, mid, lo = _split3(x)
    return _dot(hi, m01) + _dot(mid, m01) + _dot(lo, m01)


def _dot_sel_l(m01, x):
    hi, mid, lo = _split3(x)
    return _dot(m01, hi) + _dot(m01, mid) + _dot(m01, lo)


def _rms_mod(x, g, sc, sh):
    ms = jnp.mean(x * x, axis=-1, keepdims=True)
    return (x * lax.rsqrt(ms + EPS) * g) * (1.0 + sc) + sh


def _silu(x):
    return x * (1.0 / (1.0 + jnp.exp(-x)))


def _sigmoid(x):
    return 1.0 / (1.0 + jnp.exp(-x))


def _log_sigmoid(x):
    return jnp.minimum(x, 0.0) - jnp.log1p(jnp.exp(-jnp.abs(x)))


def _softplus(x):
    return jnp.maximum(x, 0.0) + jnp.log1p(jnp.exp(-jnp.abs(x)))


def _head_norm(a, seg, g):
    ss = _dot_sel_r(a * a, seg)
    return a * lax.rsqrt(ss * (1.0 / HEAD_DIM) + EPS) * g


def _ada_kernel(c_ref, w_ref, b_ref, o_ref):
    c = c_ref[...]
    s = _silu(c)
    w = w_ref[0]
    s_hi = s.astype(BF16)
    s_lo = (s - s_hi.astype(F32)).astype(BF16)
    w_hi = w.astype(BF16)
    w_lo = (w - w_hi.astype(F32)).astype(BF16)
    o_ref[0] = _dot(s_hi, w_hi) + _dot(s_hi, w_lo) + _dot(s_lo, w_hi) + b_ref[0]


def _ada_mod(c_all, ada_w, ada_b):
    depth = ada_w.shape[0]
    r = c_all.shape[0]
    tn = 1024
    return pl.pallas_call(
        _ada_kernel,
        grid=(depth, 6 * D_MODEL // tn),
        in_specs=[
            pl.BlockSpec((r, D_MODEL), lambda l, j: (0, 0)),
            pl.BlockSpec((1, D_MODEL, tn), lambda l, j: (l, 0, j)),
            pl.BlockSpec((1, 1, tn), lambda l, j: (l, 0, j)),
        ],
        out_specs=pl.BlockSpec((1, r, tn), lambda l, j: (l, 0, j)),
        out_shape=jax.ShapeDtypeStruct((depth, r, 6 * D_MODEL), F32),
        compiler_params=_cparams("parallel", "parallel"),
    )(c_all, ada_w, ada_b.reshape(depth, 1, 6 * D_MODEL))


def _mod_specs(tm, rows_per_group, mod_rows):
    tiles_per_group = rows_per_group // tm
    if mod_rows == 1:
        return pl.BlockSpec((1, 1, D_MODEL), lambda i: (i // tiles_per_group, 0, 0))
    return pl.BlockSpec((1, tm, D_MODEL), lambda i: (i // tiles_per_group, i % tiles_per_group, 0))


def _full(shape):
    nd = len(shape)
    return pl.BlockSpec(shape, lambda *_: (0,) * nd)


def _fox_proj_kernel(x_ref, sc_ref, sh_ref, g_ref, w_ref, wft_ref, fb_ref, fbt_ref, qg_ref, kg_ref, seg_ref,
                     q_out, kf_out, kb_out, vf_out, vb_out, lf_out, lft_out):
    h = _rms_mod(x_ref[...], g_ref[...], sc_ref[0], sh_ref[0]).astype(BF16)
    seg = seg_ref[...]
    fq = _dot(h, w_ref[:, 0:ATT_W])
    q_out[...] = (_head_norm(fq, seg, qg_ref[...]) * Q_SCALE).astype(BF16)
    fk = _head_norm(_dot(h, w_ref[:, ATT_W:2 * ATT_W]), seg, kg_ref[...])
    kf_out[...] = fk
    kb_out[...] = fk.astype(BF16)
    fv = _dot(h, w_ref[:, 2 * ATT_W:3 * ATT_W])
    vf_out[...] = fv
    vb_out[...] = fv.astype(BF16)
    ff = _dot(h, w_ref[:, 3 * ATT_W:3 * ATT_W + LANES])
    lf_out[...] = _log_sigmoid(ff[:, 0:FOX_HEADS] + fb_ref[...])
    fft = _dot_nt(wft_ref[...], h)
    lft_out[...] = _log_sigmoid(fft[0:FOX_HEADS, :] + fbt_ref[...])


def _fox_proj(x2, sc, sh, g, w, wft, fb, fbt, qg, kg, seg, *, tm, rows_per_group):
    t = x2.shape[0]
    mod_spec = _mod_specs(tm, rows_per_group, sc.shape[1])
    row = lambda n: pl.BlockSpec((tm, n), lambda i: (i, 0))
    return pl.pallas_call(
        _fox_proj_kernel,
        grid=(t // tm,),
        in_specs=[row(D_MODEL), mod_spec, mod_spec, _full((1, D_MODEL)), _full(w.shape), _full(wft.shape),
                  _full(fb.shape), _full(fbt.shape), _full(qg.shape), _full(kg.shape), _full(seg.shape)],
        out_specs=[row(ATT_W), row(ATT_W), row(ATT_W), row(ATT_W), row(ATT_W), row(FOX_HEADS),
                   pl.BlockSpec((FOX_HEADS, tm), lambda i: (0, i))],
        out_shape=[jax.ShapeDtypeStruct((t, ATT_W), BF16), jax.ShapeDtypeStruct((t, ATT_W), F32),
                   jax.ShapeDtypeStruct((t, ATT_W), BF16), jax.ShapeDtypeStruct((t, ATT_W), F32),
                   jax.ShapeDtypeStruct((t, ATT_W), BF16), jax.ShapeDtypeStruct((t, FOX_HEADS), F32),
                   jax.ShapeDtypeStruct((FOX_HEADS, t), F32)],
        compiler_params=_cparams("parallel"),
    )(x2, sc, sh, g, w, wft, fb, fbt, qg, kg, seg)


def _rope(a, cos, sin_signed, first_half):
    half = HEAD_DIM // 2
    chunks = []
    for c in range(a.shape[1] // LANES):
        ac = a[:, c * LANES:(c + 1) * LANES]
        chunks.append(jnp.where(first_half[:, c * LANES:(c + 1) * LANES],
                                pltpu.roll(ac, LANES - half, 1), pltpu.roll(ac, half, 1)))
    return a * cos + jnp.concatenate(chunks, axis=1) * sin_signed


def _diff_proj_kernel(x_ref, sc_ref, sh_ref, g_ref, w_ref, qg_ref, kg_ref, seg_ref, cos_ref, sin_ref,
                      q_out, kf_out, kb_out, vf_out, vb_out):
    h = _rms_mod(x_ref[...], g_ref[...], sc_ref[0], sh_ref[0]).astype(BF16)
    seg = seg_ref[...]
    cos = jnp.concatenate([cos_ref[...]] * (ATT_W // LANES), axis=1)
    sin = jnp.concatenate([sin_ref[...]] * (ATT_W // LANES), axis=1)
    lane = lax.broadcasted_iota(jnp.int32, (1, ATT_W), 1)
    first_half = (lane % HEAD_DIM) < (HEAD_DIM // 2)
    dq = _head_norm(_dot(h, w_ref[:, 0:ATT_W]), seg, qg_ref[...])
    q_out[...] = (_rope(dq, cos, sin, first_half) * Q_SCALE).astype(BF16)
    dk = _rope(_head_norm(_dot(h, w_ref[:, ATT_W:2 * ATT_W]), seg, kg_ref[...]), cos, sin, first_half)
    kf_out[...] = dk
    kb_out[...] = dk.astype(BF16)
    dv = _dot(h, w_ref[:, 2 * ATT_W:3 * ATT_W])
    vf_out[...] = dv
    vb_out[...] = dv.astype(BF16)


def _diff_proj(x2, sc, sh, g, w, qg, kg, seg, cos, sin, *, tm, rows_per_group):
    t = x2.shape[0]
    mod_spec = _mod_specs(tm, rows_per_group, sc.shape[1])
    row = lambda n: pl.BlockSpec((tm, n), lambda i: (i, 0))
    tiles_per_group = rows_per_group // tm
    pos_spec = pl.BlockSpec((tm, LANES), lambda i: (i % tiles_per_group, 0))
    return pl.pallas_call(
        _diff_proj_kernel,
        grid=(t // tm,),
        in_specs=[row(D_MODEL), mod_spec, mod_spec, _full((1, D_MODEL)), _full(w.shape),
                  _full(qg.shape), _full(kg.shape), _full(seg.shape), pos_spec, pos_spec],
        out_specs=[row(ATT_W)] * 5,
        out_shape=[jax.ShapeDtypeStruct((t, ATT_W), BF16), jax.ShapeDtypeStruct((t, ATT_W), F32),
                   jax.ShapeDtypeStruct((t, ATT_W), BF16), jax.ShapeDtypeStruct((t, ATT_W), F32),
                   jax.ShapeDtypeStruct((t, ATT_W), BF16)],
        compiler_params=_cparams("parallel"),
    )(x2, sc, sh, g, w, qg, kg, seg, cos, sin)


def _ssd_proj_kernel(x_ref, sc_ref, sh_ref, g_ref, w_ref, dtb_ref, z_out, xbc_out, dt_out):
    h = _rms_mod(x_ref[...], g_ref[...], sc_ref[0], sh_ref[0]).astype(BF16)
    z_out[...] = _dot(h, w_ref[:, 0:SSD_INNER])
    for j in range(SSD_CONV_DIM // 512):
        lo = SSD_INNER + j * 512
        xbc_out[:, j * 512:(j + 1) * 512] = _dot(h, w_ref[:, lo:lo + 512])
    lo = SSD_INNER + SSD_CONV_DIM
    dt_out[...] = _softplus(_dot(h, w_ref[:, lo:lo + LANES]) + dtb_ref[...])


def _ssd_proj(x2, sc, sh, g, w, dtb, *, tm, rows_per_group):
    t = x2.shape[0]
    mod_spec = _mod_specs(tm, rows_per_group, sc.shape[1])
    row = lambda n: pl.BlockSpec((tm, n), lambda i: (i, 0))
    return pl.pallas_call(
        _ssd_proj_kernel,
        grid=(t // tm,),
        in_specs=[row(D_MODEL), mod_spec, mod_spec, _full((1, D_MODEL)), _full(w.shape), _full(dtb.shape)],
        out_specs=[row(SSD_INNER), row(SSD_CONV_DIM), row(LANES)],
        out_shape=[jax.ShapeDtypeStruct((t, SSD_INNER), F32), jax.ShapeDtypeStruct((t, SSD_CONV_DIM), F32),
                   jax.ShapeDtypeStruct((t, LANES), F32)],
        compiler_params=_cparams("parallel"),
    )(x2, sc, sh, g, w, dtb)


def _cumsum_kernel(x_ref, tri_ref, o_ref):
    tri = tri_ref[...]
    n = x_ref.shape[1] // LANES
    carry = jnp.zeros((x_ref.shape[0], 1), F32)
    for c in range(n):
        cs = _dot_sel_r(x_ref[:, c * LANES:(c + 1) * LANES], tri) + carry
        o_ref[:, c * LANES:(c + 1) * LANES] = cs * LOG2E
        carry = cs[:, LANES - 1:LANES]


def _cumsum_rows(x, tri, seq):
    r, t = x.shape
    return pl.pallas_call(
        _cumsum_kernel,
        grid=(t // seq,),
        in_specs=[pl.BlockSpec((r, seq), lambda b: (0, b)), _full(tri.shape)],
        out_specs=pl.BlockSpec((r, seq), lambda b: (0, b)),
        out_shape=jax.ShapeDtypeStruct((r, t), F32),
        compiler_params=_cparams("parallel"),
    )(x, tri)


def _diff_lambda(dl, lam_init):
    a = jnp.sum(dl[0:1, :] * dl[1:2, :], axis=1, keepdims=True)
    b = jnp.sum(dl[2:3, :] * dl[3:4, :], axis=1, keepdims=True)
    return jnp.exp(a) - jnp.exp(b) + lam_init


def _attn_kernel(*refs, fox, tq, tk, lam_init):
    if fox:
        q_ref, k_ref, v_ref, f_ref, o_ref = refs
    else:
        q_ref, k_ref, v_ref, dl_ref, sg_ref, o_ref = refs
    qi = pl.program_id(2)
    q = q_ref[0]
    lane = lax.broadcasted_iota(jnp.int32, (1, LANES), 1)
    zero = jnp.zeros_like(q)
    q_maps = (jnp.where(lane < HEAD_DIM, q, zero), jnp.where(lane >= HEAD_DIM, q, zero))
    n_chunks = tk // LANES

    def block(kb, carry, diagonal):
        start = pl.multiple_of(kb * tk, tk)
        k = k_ref[0, pl.ds(start, tk), :]
        v = v_ref[0, pl.ds(start, tk), :]
        if diagonal:
            visible = (lax.broadcasted_iota(jnp.int32, (tq, tk), 1)
                       <= lax.broadcasted_iota(jnp.int32, (tq, tk), 0))
        out = []
        for i in range(2):
            m_prev, l_prev, acc_prev = carry[3 * i:3 * i + 3]
            s = _dot_nt(q_maps[i], k)
            if fox:
                s = s - f_ref[kb, i:i + 1, :]
            if diagonal:
                s = jnp.where(visible, s, NEG_INF)
            chunks = [s[:, c * LANES:(c + 1) * LANES] for c in range(n_chunks)]
            mx = functools.reduce(jnp.maximum, chunks)
            m_new = jnp.maximum(m_prev, jnp.broadcast_to(jnp.max(mx, axis=1, keepdims=True), (tq, LANES)))
            alpha = jnp.exp2(m_prev - m_new)
            p_chunks = [jnp.exp2(c - m_new) for c in chunks]
            psum = functools.reduce(jnp.add, p_chunks)
            l_new = alpha * l_prev + jnp.broadcast_to(jnp.sum(psum, axis=1, keepdims=True), (tq, LANES))
            p = jnp.concatenate(p_chunks, axis=1).astype(BF16)
            out += [m_new, l_new, alpha * acc_prev + _dot(p, v)]
        return tuple(out)

    init = (jnp.full((tq, LANES), NEG_INF, F32), jnp.zeros((tq, LANES), F32), jnp.zeros((tq, LANES), F32)) * 2
    carry = lax.fori_loop(0, qi, lambda kb, c: block(kb, c, False), init)
    _, l0, acc0, _, l1, acc1 = block(qi, carry, True)
    o0 = acc0 / l0
    o1 = acc1 / l1
    if fox:
        o_ref[0] = jnp.where(lane < HEAD_DIM, o0, o1).astype(BF16)
    else:
        o = o0 - _diff_lambda(dl_ref[...], lam_init) * o1
        ms = jnp.mean(o * o, axis=1, keepdims=True)
        o_ref[0] = (o * lax.rsqrt(ms + EPS) * sg_ref[...] * (1.0 - lam_init)).astype(BF16)


def _prompt_attention(q, k, v, extra, *, fox, lam_init=0.0):
    b, s, _ = q.shape
    tq = tk = min(512, s)
    nq = s // tq
    qspec = pl.BlockSpec((1, tq, LANES), lambda bi, c, qi: (bi, qi, c))
    kspec = pl.BlockSpec((1, s, LANES), lambda bi, c, qi: (bi, 0, c))
    if fox:
        (fcum,) = extra
        espec = [pl.BlockSpec((None, None, nq, 2, tk), lambda bi, c, qi: (c, bi, 0, 0, 0))]
    else:
        dl, sg = extra
        espec = [_full(dl.shape), _full(sg.shape)]
    return pl.pallas_call(
        functools.partial(_attn_kernel, fox=fox, tq=tq, tk=tk, lam_init=lam_init),
        grid=(b, ATT_W // LANES, nq),
        in_specs=[qspec, kspec, kspec] + espec,
        out_specs=qspec,
        out_shape=jax.ShapeDtypeStruct((b, s, ATT_W), BF16),
        compiler_params=_cparams("parallel", "parallel", "arbitrary"),
    )(q, k, v, *extra)


PAGES_PER_STEP = 16
N_ROWS = 32
N_NEW = 128
DIFF_NEW_KEYS = 32


def _decode_kernel(*refs, fox, n_q, lam_init):
    pps = PAGES_PER_STEP
    pos = 1
    q_ref = refs[pos]; pos += 1
    k_refs = refs[pos:pos + pps]; pos += pps
    v_refs = refs[pos:pos + pps]; pos += pps
    if fox:
        f_refs = refs[pos:pos + pps]; pos += pps
        kn_ref, vn_ref, maskn_ref, fn_ref, tri_ref = refs[pos:pos + 5]; pos += 5
    else:
        kn_ref, vn_ref, mask_ref, maskn_ref, dl_ref, sg_ref = refs[pos:pos + 6]; pos += 6
    o_ref = refs[pos]; pos += 1
    q_sc, m_sc, l_sc, acc_sc, car_sc = refs[pos:pos + 5]
    j = pl.program_id(1)

    @pl.when(j == 0)
    def _():
        q = q_ref[0]
        if fox:
            head = lax.broadcasted_iota(jnp.int32, (FOX_HEADS, ATT_W), 0)
            col = lax.broadcasted_iota(jnp.int32, (FOX_HEADS, ATT_W), 1)
            own = (col // HEAD_DIM) == head
            for qq in range(n_q):
                row = jnp.broadcast_to(q[qq:qq + 1, :], (FOX_HEADS, ATT_W))
                q_sc[qq * FOX_HEADS:(qq + 1) * FOX_HEADS, :] = jnp.where(own, row, 0.0).astype(BF16)
        else:
            lane = lax.broadcasted_iota(jnp.int32, (1, LANES), 1)
            q_sc[...] = jnp.concatenate([jnp.where(lane < HEAD_DIM, q, 0.0),
                                         jnp.where(lane >= HEAD_DIM, q, 0.0)], axis=0).astype(BF16)
        m_sc[...] = jnp.full(m_sc.shape, NEG_INF, F32)
        l_sc[...] = jnp.zeros(l_sc.shape, F32)
        acc_sc[...] = jnp.zeros(acc_sc.shape, F32)
        car_sc[...] = jnp.zeros(car_sc.shape, F32)

    def pages(k_pages, v_pages, f_pages, mask):
        scores = []
        if fox:
            carry = car_sc[...]
        for u, k_page in enumerate(k_pages):
            if fox:
                s = _dot(q_sc[...], k_page.astype(BF16))
                fe = _dot_sel_r(f_pages[u], tri_ref[...]) + carry
                carry = fe[:, PAGE - 1:PAGE]
                s = s - jnp.concatenate([fe * LOG2E] * n_q, axis=0)
            else:
                s = _dot_nt(q_sc[...], k_page.astype(BF16))
            scores.append(s if mask is None else s + mask)
        if fox:
            car_sc[...] = carry
        m_prev = m_sc[...]
        m_new = jnp.maximum(m_prev, jnp.max(functools.reduce(jnp.maximum, scores), axis=1, keepdims=True))
        alpha = jnp.exp2(m_prev - m_new)
        p_sum = None
        pv = None
        for s, v_page in zip(scores, v_pages):
            p = jnp.exp2(s - m_new)
            p_sum = p if p_sum is None else p_sum + p
            p = p.astype(BF16)
            t = _dot_nt(p, v_page.astype(BF16)) if fox else _dot(p, v_page.astype(BF16))
            pv = t if pv is None else pv + t
        l_sc[...] = alpha * l_sc[...] + jnp.sum(p_sum, axis=1, keepdims=True)
        acc_sc[...] = alpha * acc_sc[...] + pv
        m_sc[...] = m_new

    pages([r[...] for r in k_refs], [r[...] for r in v_refs], [r[...] for r in f_refs] if fox else None,
          None if fox else mask_ref[...])

    @pl.when(j == pl.num_programs(1) - 1)
    def _():
        pages([kn_ref[0]], [vn_ref[0]], [fn_ref[0]] if fox else None, maskn_ref[...])
        accn = acc_sc[...] / l_sc[...]
        if fox:
            head = lax.broadcasted_iota(jnp.int32, (FOX_HEADS, ATT_W), 0)
            col = lax.broadcasted_iota(jnp.int32, (FOX_HEADS, ATT_W), 1)
            own = jnp.where((col // HEAD_DIM) == head, 1.0, 0.0)
            o_ref[0] = jnp.concatenate(
                [jnp.sum(accn[qq * FOX_HEADS:(qq + 1) * FOX_HEADS, :] * own, axis=0, keepdims=True)
                 for qq in range(n_q)], axis=0)
        else:
            half = N_ROWS // 2
            o = accn[0:half, :] - _diff_lambda(dl_ref[...], lam_init) * accn[half:N_ROWS, :]
            ms = jnp.mean(o * o, axis=1, keepdims=True)
            o_ref[0] = o * lax.rsqrt(ms + EPS) * sg_ref[...] * (1.0 - lam_init)


def _decode_attention(page_table, layer, q, k_pool, v_pool, f_pool, k_new, v_new, f_new, extra, *, fox,
                      lam_init=0.0):
    db, q_rows, q_cols = q.shape
    n_pages = page_table.shape[1]
    pps = PAGES_PER_STEP
    assert n_pages % pps == 0
    page_rows, page_cols = k_pool.shape[2], k_pool.shape[3]
    const = lambda a: pl.BlockSpec(a.shape, lambda b, j, pt: (0,) * a.ndim)

    def pool_spec(u, rows, cols):
        return pl.BlockSpec((None, None, rows, cols), lambda b, j, pt: (layer, pt[b, j * pps + u], 0, 0))

    per_seq = lambda a: pl.BlockSpec((1,) + a.shape[1:], lambda b, j, pt: (b, 0, 0))
    in_specs = [per_seq(q)]
    args = [q]
    in_specs += [pool_spec(u, page_rows, page_cols) for u in range(pps)]
    args += [k_pool] * pps
    in_specs += [pool_spec(u, page_rows, page_cols) for u in range(pps)]
    args += [v_pool] * pps
    if fox:
        in_specs += [pool_spec(u, FOX_HEADS, PAGE) for u in range(pps)]
        args += [f_pool] * pps
        mask_new, tri = extra
        in_specs += [per_seq(k_new), per_seq(v_new), const(mask_new), per_seq(f_new), const(tri)]
        args += [k_new, v_new, mask_new, f_new, tri]
        n_q, acc_cols = q_rows, ATT_W
    else:
        in_specs += [per_seq(k_new), per_seq(v_new)] + [const(a) for a in extra]
        args += [k_new, v_new, *extra]
        n_q, acc_cols = q_rows // DIFF_HEADS, q_cols
    grid_spec = pltpu.PrefetchScalarGridSpec(
        num_scalar_prefetch=1,
        grid=(db, n_pages // pps),
        in_specs=in_specs,
        out_specs=pl.BlockSpec((1, q_rows, q_cols), lambda b, j, pt: (b, 0, 0)),
        scratch_shapes=[pltpu.VMEM((N_ROWS, acc_cols), BF16), pltpu.VMEM((N_ROWS, 1), F32),
                        pltpu.VMEM((N_ROWS, 1), F32), pltpu.VMEM((N_ROWS, acc_cols), F32),
                        pltpu.VMEM((FOX_HEADS, 1), F32)],
    )
    return pl.pallas_call(
        functools.partial(_decode_kernel, fox=fox, n_q=n_q, lam_init=lam_init),
        grid_spec=grid_spec,
        out_shape=jax.ShapeDtypeStruct((db, q_rows, q_cols), F32),
        compiler_params=_cparams("parallel", "arbitrary"),
    )(page_table, *args)


def _ssd_kernel(*refs, has_init):
    if has_init:
        (xbc_ref, z_ref, dt_ref, conv0_ref, s0_ref, cw_ref, cb_ref, alog_ref, dexp_ref, ng_ref, exp_ref, tri_ref,
         y_ref, s_out, xbuf, st_sc) = refs
    else:
        (xbc_ref, z_ref, dt_ref, conv0_ref, cw_ref, cb_ref, alog_ref, dexp_ref, ng_ref, exp_ref, tri_ref,
         y_ref, s_out, xbuf, st_sc) = refs
    L = SSD_CHUNK
    c = pl.program_id(1)

    @pl.when(c == 0)
    def _():
        xbuf[0:SUBLANES, :] = conv0_ref[0]
        if has_init:
            st_sc[...] = jnp.transpose(s0_ref[0].reshape(SSD_INNER, SSD_STATE))
        else:
            st_sc[...] = jnp.zeros(st_sc.shape, F32)

    xbuf[SUBLANES:SUBLANES + L, :] = xbc_ref[0]
    conv = cb_ref[...] + cw_ref[3:4, :] * xbuf[SUBLANES:SUBLANES + L, :]
    for kk in range(1, SSD_CONV):
        conv = conv + cw_ref[3 - kk:4 - kk, :] * xbuf[SUBLANES - kk:SUBLANES - kk + L, :]
    xbuf[0:SUBLANES, :] = xbuf[L:L + SUBLANES, :]
    xc = _silu(conv)
    xs = xc[:, 0:SSD_INNER]
    gw = SSD_GROUPS * SSD_STATE
    bm = xc[:, SSD_INNER:SSD_INNER + gw]
    cm = xc[:, SSD_INNER + gw:SSD_INNER + 2 * gw]

    dt = dt_ref[0]
    a = dt * (-jnp.exp(alog_ref[...]))
    acum = _dot_sel_l(tri_ref[...], a)
    acum_t = jnp.transpose(acum)
    total = acum[L - 1:L, :]
    expand = exp_ref[...]
    dt_e = _dot_sel_r(dt, expand)
    eacum_e = _dot_sel_r(jnp.exp(acum), expand)
    toend_e = _dot_sel_r(jnp.exp(total - acum), expand)
    etotal_e = _dot_sel_r(jnp.exp(total), expand)
    xd = xs * dt_e
    xw = (xd * toend_e).astype(BF16)
    xd_b = xd.astype(BF16)

    ii = lax.broadcasted_iota(jnp.int32, (L, L), 0)
    jj = lax.broadcasted_iota(jnp.int32, (L, L), 1)
    lower = jj <= ii
    lane = lax.broadcasted_iota(jnp.int32, (1, LANES), 1)
    heads_per_group = SSD_HEADS // SSD_GROUPS
    gcols = heads_per_group * SSD_HEAD_DIM
    y_parts = []
    for g in range(SSD_GROUPS):
        bg = bm[:, g * SSD_STATE:(g + 1) * SSD_STATE]
        cg = cm[:, g * SSD_STATE:(g + 1) * SSD_STATE].astype(BF16)
        bg_t = jnp.transpose(bg).astype(BF16)
        cb = _dot(cg, bg_t)
        st_g = st_sc[:, g * gcols:(g + 1) * gcols]
        y_off = _dot(cg, st_g.astype(BF16)) * eacum_e[:, g * gcols:(g + 1) * gcols]
        st_sc[:, g * gcols:(g + 1) * gcols] = (st_g * etotal_e[:, g * gcols:(g + 1) * gcols]
                                               + _dot(bg_t, xw[:, g * gcols:(g + 1) * gcols]))
        for pr in range(heads_per_group // 2):
            h0 = g * heads_per_group + 2 * pr
            mats = []
            for hh in (h0, h0 + 1):
                seg = acum[:, hh:hh + 1] - acum_t[hh:hh + 1, :]
                mats.append((cb * jnp.exp(jnp.where(lower, seg, NEG_INF))).astype(BF16))
            xp = xd_b[:, h0 * SSD_HEAD_DIM:(h0 + 2) * SSD_HEAD_DIM]
            zero = jnp.zeros_like(xp)
            stacked = jnp.concatenate([jnp.where(lane < SSD_HEAD_DIM, xp, zero),
                                       jnp.where(lane >= SSD_HEAD_DIM, xp, zero)], axis=0)
            y_diag = _dot(jnp.concatenate(mats, axis=1), stacked)
            y_parts.append(y_diag + y_off[:, 2 * pr * SSD_HEAD_DIM:(2 * pr + 2) * SSD_HEAD_DIM])
    y = jnp.concatenate(y_parts, axis=1) + dexp_ref[...] * xs
    y = y * _silu(z_ref[0])
    outs = []
    for g in range(SSD_GROUPS):
        yg = y[:, g * gcols:(g + 1) * gcols]
        ms = jnp.mean(yg * yg, axis=1, keepdims=True)
        outs.append(yg * lax.rsqrt(ms + EPS) * ng_ref[:, g * gcols:(g + 1) * gcols])
    y_ref[0] = jnp.concatenate(outs, axis=1).astype(BF16)

    @pl.when(c == pl.num_programs(1) - 1)
    def _():
        s_out[0] = jnp.transpose(st_sc[...]).reshape(SSD_HEADS, SSD_HEAD_DIM, SSD_STATE)


def _ssd(xbc, z, dt, conv0, s0, cw, cb, alog, dexp, ng, expand, tri):
    b, s, _ = xbc.shape
    L = SSD_CHUNK
    has_init = s0 is not None
    blk = lambda n: pl.BlockSpec((1, L, n), lambda bi, c: (bi, c, 0))
    per_b = lambda *shape: pl.BlockSpec((1,) + shape, lambda bi, c: (bi,) + (0,) * len(shape))
    in_specs = [blk(SSD_CONV_DIM), blk(SSD_INNER), blk(LANES), per_b(SUBLANES, SSD_CONV_DIM)]
    args = [xbc, z, dt, conv0]
    if has_init:
        in_specs.append(per_b(SSD_HEADS, SSD_HEAD_DIM, SSD_STATE))
        args.append(s0)
    consts = [cw, cb, alog, dexp, ng, expand, tri]
    in_specs += [_full(a.shape) for a in consts]
    args += consts
    return pl.pallas_call(
        functools.partial(_ssd_kernel, has_init=has_init),
        grid=(b, s // L),
        in_specs=in_specs,
        out_specs=[blk(SSD_INNER), per_b(SSD_HEADS, SSD_HEAD_DIM, SSD_STATE)],
        out_shape=[jax.ShapeDtypeStruct((b, s, SSD_INNER), BF16),
                   jax.ShapeDtypeStruct((b, SSD_HEADS, SSD_HEAD_DIM, SSD_STATE), F32)],
        scratch_shapes=[pltpu.VMEM((L + SUBLANES, SSD_CONV_DIM), F32), pltpu.VMEM((SSD_STATE, SSD_INNER), F32)],
        compiler_params=_cparams("parallel", "arbitrary"),
    )(*args)


def _merge_kernel(x_ref, sc_ref, sh_ref, g1_ref, ng_ref, a_ref, b_ref, d_ref, wg_ref, wa_ref, wb_ref, wd_ref,
                  wo_ref, o_ref):
    x = x_ref[...]
    h = _rms_mod(x, ng_ref[...], sc_ref[0], sh_ref[0]).astype(BF16)
    m = _sigmoid(_dot(h, wg_ref[:, 0:D_MODEL])) * _dot(a_ref[...], wa_ref[...])
    m = m + _sigmoid(_dot(h, wg_ref[:, D_MODEL:2 * D_MODEL])) * _dot(b_ref[...], wb_ref[...])
    m = m + _sigmoid(_dot(h, wg_ref[:, 2 * D_MODEL:3 * D_MODEL])) * _dot(d_ref[...], wd_ref[...])
    o_ref[...] = x + g1_ref[0] * _dot(m.astype(BF16), wo_ref[...])


def _merge(x2, sc, sh, g1, ng, o_fox, y_ssd, o_diff, wg, wa, wb, wd, wo, *, tm, rows_per_group):
    t = x2.shape[0]
    mod_spec = _mod_specs(tm, rows_per_group, sc.shape[1])
    row = lambda n: pl.BlockSpec((tm, n), lambda i: (i, 0))
    return pl.pallas_call(
        _merge_kernel,
        grid=(t // tm,),
        in_specs=[row(D_MODEL), mod_spec, mod_spec, mod_spec, _full((1, D_MODEL)), row(ATT_W), row(SSD_INNER),
                  row(ATT_W), _full(wg.shape), _full(wa.shape), _full(wb.shape), _full(wd.shape), _full(wo.shape)],
        out_specs=row(D_MODEL),
        out_shape=jax.ShapeDtypeStruct((t, D_MODEL), F32),
        compiler_params=_cparams("parallel"),
    )(x2, sc, sh, g1, ng, o_fox, y_ssd, o_diff, wg, wa, wb, wd, wo)


def _top16(x, ids):
    vals, idxs = [], []
    for _ in range(PEER_TOPK):
        m = jnp.max(x, axis=0, keepdims=True)
        idx = jnp.min(jnp.where(x == m, ids, float(2 ** 30)), axis=0, keepdims=True)
        vals.append(m)
        idxs.append(idx)
        x = jnp.where(ids == idx, NEG_INF, x)
    return jnp.concatenate(vals, axis=0), jnp.concatenate(idxs, axis=0)


PAIR_COUNTS = tuple(PEER_TOPK // (i + 1) for i in range(PEER_TOPK))
PAIR_ROWS = -(-sum(PAIR_COUNTS) // SUBLANES) * SUBLANES


def _pair_candidates(v1, v2):
    n = v1.shape[1]
    vals, ids = [], []
    for i, cnt in enumerate(PAIR_COUNTS):
        vals.append(v1[i:i + 1, :] + v2[0:cnt, :])
        ids.append((i * PEER_TOPK + lax.broadcasted_iota(jnp.int32, (cnt, n), 0)).astype(F32))
    pad = PAIR_ROWS - sum(PAIR_COUNTS)
    vals.append(jnp.full((pad, n), NEG_INF, F32))
    ids.append(jnp.full((pad, n), float(2 ** 20), F32))
    return jnp.concatenate(vals, axis=0), jnp.concatenate(ids, axis=0)


def _pick_rows(table, sel):
    out = jnp.zeros(sel.shape, F32)
    for kk in range(PEER_TOPK):
        out = jnp.where(sel == kk, table[kk:kk + 1, :], out)
    return out


GROUP = 16


def _route_kernel(x_ref, sc_ref, sh_ref, ng_ref, wq_ref, keys_ref, h_out, g_out, i1_sc, i2_sc, gw_sc, gs_sc, *, tm):
    h = _rms_mod(x_ref[...], ng_ref[...], sc_ref[0], sh_ref[0]).astype(BF16)
    h_out[...] = h
    key_ids = lax.broadcasted_iota(jnp.int32, (PEER_KEYS, tm), 0).astype(F32)
    i1_rows, i2_rows, gw_rows = [], [], []
    for hd in range(PEER_HEADS):
        tops = []
        for i in range(2):
            lo = (hd * 2 + i) * PEER_KEYS
            qh = _dot(h, wq_ref[:, lo:lo + PEER_KEYS]).astype(BF16)
            tops.append(_top16(_dot_nt(keys_ref[i], qh), key_ids))
        (v1, x1), (v2, x2) = tops
        best_v, best_c = _top16(*_pair_candidates(v1, v2))
        best_c = best_c.astype(jnp.int32)
        i1_rows.append(_pick_rows(x1, lax.shift_right_logical(best_c, 4)))
        i2_rows.append(_pick_rows(x2, lax.bitwise_and(best_c, PEER_TOPK - 1)))
        e = jnp.exp(best_v - best_v[0:1, :])
        gw_rows.append(e / jnp.sum(e, axis=0, keepdims=True))
    i1_sc[...] = jnp.transpose(jnp.concatenate(i1_rows, axis=0))
    i2_sc[...] = jnp.transpose(jnp.concatenate(i2_rows, axis=0))
    gw_sc[...] = jnp.transpose(jnp.concatenate(gw_rows, axis=0))

    key_id = lax.broadcasted_iota(jnp.int32, (PEER_KEYS, LANES), 0).astype(F32)

    def per_group(gi, carry):
        base = pl.multiple_of(gi * GROUP, GROUP)

        def per_token(tt, c):
            r1 = i1_sc[pl.ds(base + tt, 1), :]
            r2 = i2_sc[pl.ds(base + tt, 1), :]
            rg = gw_sc[pl.ds(base + tt, 1), :]
            a = jnp.where(key_id == r1, rg, 0.0).astype(BF16)
            bsel = jnp.where(key_id == r2, 1.0, 0.0).astype(BF16)
            gs_sc[pl.ds(pl.multiple_of(tt * PEER_KEYS, PEER_KEYS), PEER_KEYS), :] = _dot_nt(a, bsel)
            return c

        lax.fori_loop(0, GROUP, per_token, 0, unroll=8)
        for k1 in range(PEER_KEYS):
            g_out[gi, k1] = gs_sc[pl.ds(k1, GROUP, stride=PEER_KEYS), :].astype(BF16)
        return carry

    lax.fori_loop(0, tm // GROUP, per_group, 0)


def _route(x2, sc, sh, ng, wq, keys, *, tm, rows_per_group):
    t = x2.shape[0]
    mod_spec = _mod_specs(tm, rows_per_group, sc.shape[1])
    row = lambda n: pl.BlockSpec((tm, n), lambda i: (i, 0))
    return pl.pallas_call(
        functools.partial(_route_kernel, tm=tm),
        grid=(t // tm,),
        in_specs=[row(D_MODEL), mod_spec, mod_spec, _full((1, D_MODEL)), _full(wq.shape), _full(keys.shape)],
        out_specs=[row(D_MODEL),
                   pl.BlockSpec((tm // GROUP, PEER_KEYS, GROUP, PEER_KEYS), lambda i: (i, 0, 0, 0))],
        out_shape=[jax.ShapeDtypeStruct((t, D_MODEL), BF16),
                   jax.ShapeDtypeStruct((t // GROUP, PEER_KEYS, GROUP, PEER_KEYS), BF16)],
        scratch_shapes=[pltpu.VMEM((tm, LANES), F32)] * 3 + [pltpu.VMEM((GROUP * PEER_KEYS, PEER_KEYS), F32)],
        compiler_params=_cparams("parallel"),
    )(x2, sc, sh, ng, wq, keys)


def _gelu(a):
    return 0.5 * a * (1.0 + lax.erf(a * (2.0 ** -0.5)))


def _expert_kernel(h_ref, g_ref, u_ref, v_ref, x_ref, g2_ref, o_ref, acc_sc):
    j = pl.program_id(1)

    @pl.when(j == 0)
    def _():
        acc_sc[...] = jnp.zeros(acc_sc.shape, F32)

    a = _dot_nt(h_ref[...], u_ref[...])
    tm = a.shape[0]
    parts = []
    for s in range(g_ref.shape[1]):
        g = g_ref[:, s].reshape(tm, PEER_KEYS).astype(F32)
        parts.append((g * _gelu(a[:, s * PEER_KEYS:(s + 1) * PEER_KEYS])).astype(BF16))
    acc_sc[...] += _dot(jnp.concatenate(parts, axis=1), v_ref[...])

    @pl.when(j == pl.num_programs(1) - 1)
    def _():
        o_ref[...] = x_ref[...] + g2_ref[0] * acc_sc[...]


def _experts(h2, gates, u, v, x2, g2, *, tm, rows_per_group, ec=1024):
    t = h2.shape[0]
    tiles_per_group = rows_per_group // tm
    if g2.shape[1] == 1:
        mod_spec = pl.BlockSpec((1, 1, D_MODEL), lambda i, j: (i // tiles_per_group, 0, 0))
    else:
        mod_spec = pl.BlockSpec((1, tm, D_MODEL), lambda i, j: (i // tiles_per_group, i % tiles_per_group, 0))
    row = pl.BlockSpec((tm, D_MODEL), lambda i, j: (i, 0))
    return pl.pallas_call(
        _expert_kernel,
        grid=(t // tm, PEER_EXPERTS // ec),
        in_specs=[row, pl.BlockSpec((tm // GROUP, ec // PEER_KEYS, GROUP, PEER_KEYS), lambda i, j: (i, j, 0, 0)),
                  pl.BlockSpec((ec, D_MODEL), lambda i, j: (j, 0)),
                  pl.BlockSpec((ec, D_MODEL), lambda i, j: (j, 0)), row, mod_spec],
        out_specs=row,
        out_shape=jax.ShapeDtypeStruct((t, D_MODEL), F32),
        scratch_shapes=[pltpu.VMEM((tm, D_MODEL), F32)],
        compiler_params=_cparams("parallel", "arbitrary"),
    )(h2, gates, u, v, x2, g2)


def _consts():
    r = np.arange(ATT_W)
    seg = (r[:, None] // HEAD_DIM == r[None, :] // HEAD_DIM).astype(np.float32)
    i = np.arange(LANES)
    tri_incl = (i[:, None] <= i[None, :]).astype(np.float32)
    tri_time = (i[None, :] <= i[:, None]).astype(np.float32)
    expand = (np.arange(SSD_INNER)[None, :] // SSD_HEAD_DIM == i[:, None]).astype(np.float32)
    return (jnp.asarray(seg, BF16), jnp.asarray(tri_incl, BF16), jnp.asarray(tri_time, BF16),
            jnp.asarray(expand, BF16))


def _decode_consts(n_q):
    to_add = lambda ok: jnp.asarray(np.where(ok, 0.0, -np.inf).astype(np.float32))
    r = np.arange(N_ROWS)[:, None]
    key = np.arange(N_NEW)[None, :]
    fox_new = to_add((key <= r // FOX_HEADS) & (key < n_q))
    head, query = r % DIFF_HEADS, (r % (N_ROWS // 2)) // DIFF_HEADS
    c = np.arange(PAGE * DIFF_HEADS)[None, :]
    diff_past = to_add(c % DIFF_HEADS == head)
    cn = np.arange(DIFF_NEW_KEYS * DIFF_HEADS)[None, :]
    kn = cn // DIFF_HEADS
    diff_new = to_add((cn % DIFF_HEADS == head) & (kn <= query) & (kn < n_q))
    return fox_new, diff_past, diff_new


def _rope_tables(positions):
    half = HEAD_DIM // 2
    inv = ROPE_THETA ** (-jnp.arange(half, dtype=F32) / half)
    ang = positions.astype(F32)[:, None] * inv[None, :]
    cos, sin = jnp.cos(ang), jnp.sin(ang)
    cos64 = jnp.concatenate([cos, cos], axis=1)
    sin64 = jnp.concatenate([-sin, sin], axis=1)
    return jnp.concatenate([cos64, cos64], axis=1), jnp.concatenate([sin64, sin64], axis=1)


def _pad_cols(a, n):
    return jnp.pad(a, ((0, 0), (0, n - a.shape[1])))


def _layer_weights(p, l):
    w_in = p['w_in'][l]
    o = IN_OFF
    col = lambda i: w_in[:, o[i]:o[i + 1]]
    w_fox = jnp.concatenate([col(0), col(1), col(2), _pad_cols(col(3), LANES)], axis=1).astype(BF16)
    wft = jnp.pad(col(3).T, ((0, 16 - FOX_HEADS), (0, 0))).astype(BF16)
    w_ssd = jnp.concatenate([col(4), col(5), _pad_cols(col(6), LANES)], axis=1).astype(BF16)
    w_diff = jnp.concatenate([col(7), col(8), col(9)], axis=1).astype(BF16)
    tile8 = lambda g: jnp.tile(g, ATT_W // HEAD_DIM)[None, :]
    return dict(
        w_fox=w_fox, wft=wft, w_ssd=w_ssd, w_diff=w_diff, w_gate=col(10).astype(BF16),
        fb=p['fox_f_b'][l][None, :], fbt=p['fox_f_b'][l][:, None],
        fox_qg=tile8(p['fox_qn_g'][l]), fox_kg=tile8(p['fox_kn_g'][l]),
        diff_qg=tile8(p['diff_qn_g'][l]), diff_kg=tile8(p['diff_kn_g'][l]),
        dtb=_pad_cols(p['dt_bias'][l][None, :], LANES),
        alog=jnp.pad(p['a_log'][l][None, :], ((0, 0), (0, LANES - SSD_HEADS)), constant_values=NEG_INF),
        dexp=jnp.repeat(p['ssd_d'][l], SSD_HEAD_DIM)[None, :],
        ssd_ng=p['ssd_norm_g'][l][None, :],
        conv_w=p['conv_w'][l], conv_b=p['conv_b'][l][None, :],
        dl=p['diff_lam'][l], sg=p['diff_subln_g'][l][None, :],
        norm1=p['norm1_g'][l][None, :], norm2=p['norm2_g'][l][None, :],
        wa=p['w_br_fox'][l].astype(BF16), wb=p['w_br_ssd'][l].astype(BF16), wd=p['w_br_diff'][l].astype(BF16),
        wo=p['w_out'][l].astype(BF16), wq=p['peer_wq'][l].astype(BF16), keys=p['peer_keys'][l].astype(BF16),
        u=p['peer_u'][l].astype(BF16), v=p['peer_v'][l].astype(BF16),
    )


def _layer(x, mods, w, consts, layer_idx, past):
    seg, tri_incl, tri_time, expand = consts
    b, s, _ = x.shape
    t = b * s
    sh1, sc1, g1, sh2, sc2, g2 = mods
    x2 = x.reshape(t, D_MODEL)
    if past is None:
        rows_per_group = s
        tm = min(512, s)
        p0 = 0
    else:
        rows_per_group = t
        tm = t
        p0 = past['page_table'].shape[1] * PAGE
    kw = dict(tm=tm, rows_per_group=rows_per_group)
    lam_init = 0.8 - 0.6 * math.exp(-0.3 * layer_idx)

    q_f, kf32, kfb, vf32, vfb, logf, logf_t = _fox_proj(
        x2, sc1, sh1, w['norm1'], w['w_fox'], w['wft'], w['fb'], w['fbt'], w['fox_qg'], w['fox_kg'], seg, **kw)
    z, xbc, dt = _ssd_proj(x2, sc1, sh1, w['norm1'], w['w_ssd'], w['dtb'], **kw)
    pos = p0 + jnp.arange(s, dtype=jnp.int32)
    if past is not None:
        pos = jnp.tile(pos, b)
    cos, sin = _rope_tables(pos)
    q_d, kd32, kdb, vd32, vdb = _diff_proj(
        x2, sc1, sh1, w['norm1'], w['w_diff'], w['diff_qg'], w['diff_kg'], seg, cos, sin, **kw)

    if past is None:
        tk = min(512, s)
        fcum = _cumsum_rows(logf_t, tri_incl, s).reshape(FOX_HEADS // 2, 2, b, s // tk, tk).transpose(0, 2, 3, 1, 4)
        o_fox =_prompt_attention(q_f.reshape(b, s, ATT_W), kfb.reshape(b, s, ATT_W), vfb.reshape(b, s, ATT_W),
                                  (fcum,), fox=True)
        o_diff = _prompt_attention(q_d.reshape(b, s, ATT_W), kdb.reshape(b, s, ATT_W), vdb.reshape(b, s, ATT_W),
                                   (w['dl'], w['sg']), fox=False, lam_init=lam_init)
        y, new_ssm = _ssd(xbc.reshape(b, s, -1), z.reshape(b, s, -1), dt.reshape(b, s, -1),
                          jnp.zeros((b, SUBLANES, SSD_CONV_DIM), F32), None,
                          w['conv_w'], w['conv_b'], w['alog'], w['dexp'], w['ssd_ng'], expand, tri_time)
        new_conv = xbc.reshape(b, s, -1)[:, s - (SSD_CONV - 1):, :]
    else:
        pad_rows = lambda a: jnp.pad(a.reshape(b, s, -1), ((0, 0), (0, PAGE - s), (0, 0)))
        f_new = jnp.pad(logf_t.reshape(FOX_HEADS, b, s).transpose(1, 0, 2), ((0, 0), (0, 0), (0, PAGE - s)))
        assert s * FOX_HEADS == N_ROWS
        fox_new, diff_past, diff_new = _decode_consts(s)
        fox_page = lambda a: jnp.pad(a.reshape(b, s, ATT_W).transpose(0, 2, 1), ((0, 0), (0, 0), (0, N_NEW - s)))
        diff_rows = lambda a: a.astype(F32).reshape(b, s * DIFF_HEADS, LANES)
        diff_page = lambda a: jnp.pad(diff_rows(a), ((0, 0), (0, (DIFF_NEW_KEYS - s) * DIFF_HEADS), (0, 0)))
        o_fox = _decode_attention(past['page_table'], layer_idx, q_f.astype(F32).reshape(b, s, ATT_W),
                                  past['fox_k'], past['fox_v'], past['fox_f'], fox_page(kf32), fox_page(vf32),
                                  f_new, (fox_new, tri_incl), fox=True)
        o_diff = _decode_attention(past['page_table'], layer_idx, diff_rows(q_d),
                                   past['diff_k'], past['diff_v'], None, diff_page(kd32), diff_page(vd32),
                                   None, (diff_past, diff_new, w['dl'], w['sg']), fox=False, lam_init=lam_init)
        o_fox = o_fox.astype(BF16)
        o_diff = o_diff.astype(BF16)
        conv0 = jnp.pad(past['conv'][layer_idx], ((0, 0), (SUBLANES - (SSD_CONV - 1), 0), (0, 0)))
        y, new_ssm = _ssd(pad_rows(xbc), pad_rows(z), pad_rows(dt), conv0, past['ssm'][layer_idx],
                          w['conv_w'], w['conv_b'], w['alog'], w['dexp'], w['ssd_ng'], expand, tri_time)
        y = y[:, :s, :]
        xbc_all = jnp.concatenate([past['conv'][layer_idx], xbc.reshape(b, s, -1)], axis=1)
        new_conv = xbc_all[:, -(SSD_CONV - 1):, :]

    x1 = _merge(x2, sc1, sh1, g1, w['norm1'], o_fox.reshape(t, ATT_W), y.reshape(t, SSD_INNER),
                o_diff.reshape(t, ATT_W), w['w_gate'], w['wa'], w['wb'], w['wd'], w['wo'], **kw)

    tm_r = min(256, tm)
    h2, gates = _route(x1, sc2, sh2, w['norm2'], w['wq'], w['keys'], tm=tm_r, rows_per_group=rows_per_group)
    x_out = _experts(h2, gates, w['u'], w['v'], x1, g2, **kw)

    rows = (kf32.reshape(b, s, FOX_HEADS, HEAD_DIM), vf32.reshape(b, s, FOX_HEADS, HEAD_DIM),
            logf.reshape(b, s, FOX_HEADS), kd32.reshape(b, s, DIFF_HEADS, 2 * HEAD_DIM),
            vd32.reshape(b, s, DIFF_HEADS, 2 * HEAD_DIM), new_ssm, new_conv)
    return x_out.reshape(b, s, D_MODEL), rows


def kernel(x_prompt, x_sample, c_prompt, c_sample, cache_fox_k, cache_fox_v, cache_fox_logf, cache_diff_k,
           cache_diff_v, state_ssm, state_conv, page_table, ada_w, ada_b, norm1_g, norm2_g, w_in, fox_f_b,
           fox_qn_g, fox_kn_g, conv_w, conv_b, dt_bias, a_log, ssd_d, ssd_norm_g, diff_qn_g, diff_kn_g,
           diff_lam, diff_subln_g, w_br_fox, w_br_ssd, w_br_diff, w_out, peer_wq, peer_keys, peer_u, peer_v):
    params = dict(w_in=w_in, fox_f_b=fox_f_b, fox_qn_g=fox_qn_g, fox_kn_g=fox_kn_g, conv_w=conv_w, conv_b=conv_b,
                  dt_bias=dt_bias, a_log=a_log, ssd_d=ssd_d, ssd_norm_g=ssd_norm_g, diff_qn_g=diff_qn_g,
                  diff_kn_g=diff_kn_g, diff_lam=diff_lam, diff_subln_g=diff_subln_g, w_br_fox=w_br_fox,
                  w_br_ssd=w_br_ssd, w_br_diff=w_br_diff, w_out=w_out, peer_wq=peer_wq, peer_keys=peer_keys,
                  peer_u=peer_u, peer_v=peer_v, norm1_g=norm1_g, norm2_g=norm2_g)
    depth = w_in.shape[0]
    bp, sp, _ = x_prompt.shape
    bs, ss, _ = x_sample.shape
    consts = _consts()

    n_c = bp + bs
    c_all = jnp.pad(jnp.concatenate([c_prompt, c_sample], axis=0), ((0, (-n_c) % SUBLANES), (0, 0)))
    mod = _ada_mod(c_all, ada_w, ada_b)

    n_pool = cache_fox_k.shape[1]
    past = dict(
        page_table=page_table,
        fox_k=cache_fox_k.transpose(0, 1, 3, 4, 2).reshape(depth, n_pool, ATT_W, PAGE),
        fox_v=cache_fox_v.transpose(0, 1, 3, 4, 2).reshape(depth, n_pool, ATT_W, PAGE),
        fox_f=jnp.swapaxes(cache_fox_logf, 2, 3),
        diff_k=cache_diff_k.reshape(depth, n_pool, PAGE * DIFF_HEADS, 2 * HEAD_DIM),
        diff_v=cache_diff_v.reshape(depth, n_pool, PAGE * DIFF_HEADS, 2 * HEAD_DIM),
        ssm=state_ssm, conv=state_conv)

    xp, xs = x_prompt, x_sample
    rows_p, rows_s = [], []
    for l in range(depth):
        w = _layer_weights(params, l)
        mp = [mod[l, :bp, i * D_MODEL:(i + 1) * D_MODEL][:, None, :] for i in range(6)]
        ms = [jnp.repeat(mod[l, bp:n_c, i * D_MODEL:(i + 1) * D_MODEL], ss, axis=0)[None] for i in range(6)]
        xp, rp = _layer(xp, mp, w, consts, l, None)
        xs, rs = _layer(xs, ms, w, consts, l, past)
        rows_p.append(rp)
        rows_s.append(rs)
    stk = lambda rows, i: jnp.stack([r[i] for r in rows])
    return (xp, xs, stk(rows_p, 0), stk(rows_s, 0), stk(rows_p, 1), stk(rows_s, 1), stk(rows_p, 2), stk(rows_s, 2),
            stk(rows_p, 3), stk(rows_s, 3), stk(rows_p, 4), stk(rows_s, 4), stk(rows_p, 5), stk(rows_s, 5),
            stk(rows_p, 6), stk(rows_s, 6))
```

```python
import functools
import math

import numpy as np
import jax
import jax.numpy as jnp
from jax import lax
from jax.experimental import pallas as pl
from jax.experimental.pallas import tpu as pltpu

F32 = jnp.float32
BF16 = jnp.bfloat16

D_MODEL = 1024
PAGE = 128
FOX_HEADS = 8
HEAD_DIM = 64
ATT_W = 512
DIFF_HEADS = 4
SSD_INNER = 1024
SSD_HEADS = 16
SSD_HEAD_DIM = 64
SSD_GROUPS = 4
SSD_STATE = 128
SSD_CONV = 4
SSD_CONV_DIM = 2048
SSD_CHUNK = 128
PEER_HEADS = 8
PEER_KEYS = 128
PEER_EXPERTS = PEER_KEYS * PEER_KEYS
PEER_TOPK = 16
PEER_QDIM = 256
ROPE_THETA = 10000.0
EPS = 1e-6
IN_SPLITS = (512, 512, 512, 8, 1024, 2048, 16, 512, 512, 512, 3072)
IN_OFF = tuple(int(v) for v in np.cumsum((0,) + IN_SPLITS))

LANES = 128
SUBLANES = 8
VMEM_LIMIT = 56 * 1024 * 1024

NEG_INF = float("-inf")
LOG2E = math.log2(math.e)
Q_SCALE = HEAD_DIM ** -0.5 * LOG2E


def _cparams(*sem):
    return pltpu.CompilerParams(dimension_semantics=sem, vmem_limit_bytes=VMEM_LIMIT)


def _dot(a, b):
    return jnp.dot(a, b, preferred_element_type=F32)


def _dot_nt(a, b):
    return lax.dot_general(a, b, (((1,), (1,)), ((), ())), preferred_element_type=F32)


def _split3(x):
    hi = x.astype(BF16)
    r1 = x - hi.astype(F32)
    mid = r1.astype(BF16)
    lo = (r1 - mid.astype(F32)).astype(BF16)
    return hi, mid, lo


def _dot_sel_r(x, m01):
    hi, mid, lo = _split3(x)
    return _dot(hi, m01) + _dot(mid, m01) + _dot(lo, m01)


def _dot_sel_l(m01, x):
    hi, mid, lo = _split3(x)
    return _dot(m01, hi) + _dot(m01, mid) + _dot(m01, lo)


def _rms_mod(x, g, sc, sh):
    ms = jnp.mean(x * x, axis=-1, keepdims=True)
    return (x * lax.rsqrt(ms + EPS) * g) * (1.0 + sc) + sh


def _silu(x):
    return x * (1.0 / (1.0 + jnp.exp(-x)))


def _sigmoid(x):
    return 1.0 / (1.0 + jnp.exp(-x))


def _log_sigmoid(x):
    return jnp.minimum(x, 0.0) - jnp.log1p(jnp.exp(-jnp.abs(x)))


def _softplus(x):
    return jnp.maximum(x, 0.0) + jnp.log1p(jnp.exp(-jnp.abs(x)))


def _head_norm(a, seg, g):
    ss = _dot_sel_r(a * a, seg)
    return a * lax.rsqrt(ss * (1.0 / HEAD_DIM) + EPS) * g


def _ada_kernel(c_ref, w_ref, b_ref, o_ref):
    c = c_ref[...]
    s = _silu(c)
    w = w_ref[0]
    s_hi = s.astype(BF16)
    s_lo = (s - s_hi.astype(F32)).astype(BF16)
    w_hi = w.astype(BF16)
    w_lo = (w - w_hi.astype(F32)).astype(BF16)
    o_ref[0] = _dot(s_hi, w_hi) + _dot(s_hi, w_lo) + _dot(s_lo, w_hi) + b_ref[0]


def _ada_mod(c_all, ada_w, ada_b):
    depth = ada_w.shape[0]
    r = c_all.shape[0]
    tn = 1024
    return pl.pallas_call(
        _ada_kernel,
        grid=(depth, 6 * D_MODEL // tn),
        in_specs=[
            pl.BlockSpec((r, D_MODEL), lambda l, j: (0, 0)),
            pl.BlockSpec((1, D_MODEL, tn), lambda l, j: (l, 0, j)),
            pl.BlockSpec((1, 1, tn), lambda l, j: (l, 0, j)),
        ],
        out_specs=pl.BlockSpec((1, r, tn), lambda l, j: (l, 0, j)),
        out_shape=jax.ShapeDtypeStruct((depth, r, 6 * D_MODEL), F32),
        compiler_params=_cparams("parallel", "parallel"),
    )(c_all, ada_w, ada_b.reshape(depth, 1, 6 * D_MODEL))


def _mod_specs(tm, rows_per_group, mod_rows):
    tiles_per_group = rows_per_group // tm
    if mod_rows == 1:
        return pl.BlockSpec((1, 1, D_MODEL), lambda i: (i // tiles_per_group, 0, 0))
    return pl.BlockSpec((1, tm, D_MODEL), lambda i: (i // tiles_per_group, i % tiles_per_group, 0))


def _full(shape):
    nd = len(shape)
    return pl.BlockSpec(shape, lambda *_: (0,) * nd)


def _fox_proj_kernel(x_ref, sc_ref, sh_ref, g_ref, w_ref, wft_ref, fb_ref, fbt_ref, qg_ref, kg_ref, seg_ref,
                     q_out, kf_out, kb_out, vf_out, vb_out, lf_out, lft_out):
    h = _rms_mod(x_ref[...], g_ref[...], sc_ref[0], sh_ref[0]).astype(BF16)
    seg = seg_ref[...]
    fq = _dot(h, w_ref[:, 0:ATT_W])
    q_out[...] = (_head_norm(fq, seg, qg_ref[...]) * Q_SCALE).astype(BF16)
    fk = _head_norm(_dot(h, w_ref[:, ATT_W:2 * ATT_W]), seg, kg_ref[...])
    kf_out[...] = fk
    kb_out[...] = fk.astype(BF16)
    fv = _dot(h, w_ref[:, 2 * ATT_W:3 * ATT_W])
    vf_out[...] = fv
    vb_out[...] = fv.astype(BF16)
    ff = _dot(h, w_ref[:, 3 * ATT_W:3 * ATT_W + LANES])
    lf_out[...] = _log_sigmoid(ff[:, 0:FOX_HEADS] + fb_ref[...])
    fft = _dot_nt(wft_ref[...], h)
    lft_out[...] = _log_sigmoid(fft[0:FOX_HEADS, :] + fbt_ref[...])


def _fox_proj(x2, sc, sh, g, w, wft, fb, fbt, qg, kg, seg, *, tm, rows_per_group):
    t = x2.shape[0]
    mod_spec = _mod_specs(tm, rows_per_group, sc.shape[1])
    row = lambda n: pl.BlockSpec((tm, n), lambda i: (i, 0))
    return pl.pallas_call(
        _fox_proj_kernel,
        grid=(t // tm,),
        in_specs=[row(D_MODEL), mod_spec, mod_spec, _full((1, D_MODEL)), _full(w.shape), _full(wft.shape),
                  _full(fb.shape), _full(fbt.shape), _full(qg.shape), _full(kg.shape), _full(seg.shape)],
        out_specs=[row(ATT_W), row(ATT_W), row(ATT_W), row(ATT_W), row(ATT_W), row(FOX_HEADS),
                   pl.BlockSpec((FOX_HEADS, tm), lambda i: (0, i))],
        out_shape=[jax.ShapeDtypeStruct((t, ATT_W), BF16), jax.ShapeDtypeStruct((t, ATT_W), F32),
                   jax.ShapeDtypeStruct((t, ATT_W), BF16), jax.ShapeDtypeStruct((t, ATT_W), F32),
                   jax.ShapeDtypeStruct((t, ATT_W), BF16), jax.ShapeDtypeStruct((t, FOX_HEADS), F32),
                   jax.ShapeDtypeStruct((FOX_HEADS, t), F32)],
        compiler_params=_cparams("parallel"),
    )(x2, sc, sh, g, w, wft, fb, fbt, qg, kg, seg)


def _rope(a, cos, sin_signed, first_half):
    half = HEAD_DIM // 2
    chunks = []
    for c in range(a.shape[1] // LANES):
        ac = a[:, c * LANES:(c + 1) * LANES]
        chunks.append(jnp.where(first_half[:, c * LANES:(c + 1) * LANES],
                                pltpu.roll(ac, LANES - half, 1), pltpu.roll(ac, half, 1)))
    return a * cos + jnp.concatenate(chunks, axis=1) * sin_signed


def _diff_proj_kernel(x_ref, sc_ref, sh_ref, g_ref, w_ref, qg_ref, kg_ref, seg_ref, cos_ref, sin_ref,
                      q_out, kf_out, kb_out, vf_out, vb_out):
    h = _rms_mod(x_ref[...], g_ref[...], sc_ref[0], sh_ref[0]).astype(BF16)
    seg = seg_ref[...]
    cos = jnp.concatenate([cos_ref[...]] * (ATT_W // LANES), axis=1)
    sin = jnp.concatenate([sin_ref[...]] * (ATT_W // LANES), axis=1)
    lane = lax.broadcasted_iota(jnp.int32, (1, ATT_W), 1)
    first_half = (lane % HEAD_DIM) < (HEAD_DIM // 2)
    dq = _head_norm(_dot(h, w_ref[:, 0:ATT_W]), seg, qg_ref[...])
    q_out[...] = (_rope(dq, cos, sin, first_half) * Q_SCALE).astype(BF16)
    dk = _rope(_head_norm(_dot(h, w_ref[:, ATT_W:2 * ATT_W]), seg, kg_ref[...]), cos, sin, first_half)
    kf_out[...] = dk
    kb_out[...] = dk.astype(BF16)
    dv = _dot(h, w_ref[:, 2 * ATT_W:3 * ATT_W])
    vf_out[...] = dv
    vb_out[...] = dv.astype(BF16)


def _diff_proj(x2, sc, sh, g, w, qg, kg, seg, cos, sin, *, tm, rows_per_group):
    t = x2.shape[0]
    mod_spec = _mod_specs(tm, rows_per_group, sc.shape[1])
    row = lambda n: pl.BlockSpec((tm, n), lambda i: (i, 0))
    tiles_per_group = rows_per_group // tm
    pos_spec = pl.BlockSpec((tm, LANES), lambda i: (i % tiles_per_group, 0))
    return pl.pallas_call(
        _diff_proj_kernel,
        grid=(t // tm,),
        in_specs=[row(D_MODEL), mod_spec, mod_spec, _full((1, D_MODEL)), _full(w.shape),
                  _full(qg.shape), _full(kg.shape), _full(seg.shape), pos_spec, pos_spec],
        out_specs=[row(ATT_W)] * 5,
        out_shape=[jax.ShapeDtypeStruct((t, ATT_W), BF16), jax.ShapeDtypeStruct((t, ATT_W), F32),
                   jax.ShapeDtypeStruct((t, ATT_W), BF16), jax.ShapeDtypeStruct((t, ATT_W), F32),
                   jax.ShapeDtypeStruct((t, ATT_W), BF16)],
        compiler_params=_cparams("parallel"),
    )(x2, sc, sh, g, w, qg, kg, seg, cos, sin)


def _ssd_proj_kernel(x_ref, sc_ref, sh_ref, g_ref, w_ref, dtb_ref, z_out, xbc_out, dt_out):
    h = _rms_mod(x_ref[...], g_ref[...], sc_ref[0], sh_ref[0]).astype(BF16)
    z_out[...] = _dot(h, w_ref[:, 0:SSD_INNER])
    for j in range(SSD_CONV_DIM // 512):
        lo = SSD_INNER + j * 512
        xbc_out[:, j * 512:(j + 1) * 512] = _dot(h, w_ref[:, lo:lo + 512])
    lo = SSD_INNER + SSD_CONV_DIM
    dt_out[...] = _softplus(_dot(h, w_ref[:, lo:lo + LANES]) + dtb_ref[...])


def _ssd_proj(x2, sc, sh, g, w, dtb, *, tm, rows_per_group):
    t = x2.shape[0]
    mod_spec = _mod_specs(tm, rows_per_group, sc.shape[1])
    row = lambda n: pl.BlockSpec((tm, n), lambda i: (i, 0))
    return pl.pallas_call(
        _ssd_proj_kernel,
        grid=(t // tm,),
        in_specs=[row(D_MODEL), mod_spec, mod_spec, _full((1, D_MODEL)), _full(w.shape), _full(dtb.shape)],
        out_specs=[row(SSD_INNER), row(SSD_CONV_DIM), row(LANES)],
        out_shape=[jax.ShapeDtypeStruct((t, SSD_INNER), F32), jax.ShapeDtypeStruct((t, SSD_CONV_DIM), F32),
                   jax.ShapeDtypeStruct((t, LANES), F32)],
        compiler_params=_cparams("parallel"),
    )(x2, sc, sh, g, w, dtb)


def _cumsum_kernel(x_ref, tri_ref, o_ref):
    tri = tri_ref[...]
    n = x_ref.shape[1] // LANES
    carry = jnp.zeros((x_ref.shape[0], 1), F32)
    for c in range(n):
        cs = _dot_sel_r(x_ref[:, c * LANES:(c + 1) * LANES], tri) + carry
        o_ref[:, c * LANES:(c + 1) * LANES] = cs * LOG2E
        carry = cs[:, LANES - 1:LANES]


def _cumsum_rows(x, tri, seq):
    r, t = x.shape
    return pl.pallas_call(
        _cumsum_kernel,
        grid=(t // seq,),
        in_specs=[pl.BlockSpec((r, seq), lambda b: (0, b)), _full(tri.shape)],
        out_specs=pl.BlockSpec((r, seq), lambda b: (0, b)),
        out_shape=jax.ShapeDtypeStruct((r, t), F32),
        compiler_params=_cparams("parallel"),
    )(x, tri)


def _diff_lambda(dl, lam_init):
    a = jnp.sum(dl[0:1, :] * dl[1:2, :], axis=1, keepdims=True)
    b = jnp.sum(dl[2:3, :] * dl[3:4, :], axis=1, keepdims=True)
    return jnp.exp(a) - jnp.exp(b) + lam_init


def _attn_kernel(*refs, fox, tq, tk, lam_init):
    if fox:
        q_ref, k_ref, v_ref, f_ref, o_ref = refs
    else:
        q_ref, k_ref, v_ref, dl_ref, sg_ref, o_ref = refs
    qi = pl.program_id(2)
    q = q_ref[0]
    lane = lax.broadcasted_iota(jnp.int32, (1, LANES), 1)
    zero = jnp.zeros_like(q)
    q_maps = (jnp.where(lane < HEAD_DIM, q, zero), jnp.where(lane >= HEAD_DIM, q, zero))
    n_chunks = tk // LANES

    def block(kb, carry, diagonal):
        start = pl.multiple_of(kb * tk, tk)
        k = k_ref[0, pl.ds(start, tk), :]
        v = v_ref[0, pl.ds(start, tk), :]
        if diagonal:
            visible = (lax.broadcasted_iota(jnp.int32, (tq, tk), 1)
                       <= lax.broadcasted_iota(jnp.int32, (tq, tk), 0))
        out = []
        for i in range(2):
            m_prev, l_prev, acc_prev = carry[3 * i:3 * i + 3]
            s = _dot_nt(q_maps[i], k)
            if fox:
                s = s - f_ref[kb, i:i + 1, :]
            if diagonal:
                s = jnp.where(visible, s, NEG_INF)
            chunks = [s[:, c * LANES:(c + 1) * LANES] for c in range(n_chunks)]
            mx = functools.reduce(jnp.maximum, chunks)
            m_new = jnp.maximum(m_prev, jnp.broadcast_to(jnp.max(mx, axis=1, keepdims=True), (tq, LANES)))
            alpha = jnp.exp2(m_prev - m_new)
            p_chunks = [jnp.exp2(c - m_new) for c in chunks]
            psum = functools.reduce(jnp.add, p_chunks)
            l_new = alpha * l_prev + jnp.broadcast_to(jnp.sum(psum, axis=1, keepdims=True), (tq, LANES))
            p = jnp.concatenate(p_chunks, axis=1).astype(BF16)
            out += [m_new, l_new, alpha * acc_prev + _dot(p, v)]
        return tuple(out)

    init = (jnp.full((tq, LANES), NEG_INF, F32), jnp.zeros((tq, LANES), F32), jnp.zeros((tq, LANES), F32)) * 2
    carry = lax.fori_loop(0, qi, lambda kb, c: block(kb, c, False), init)
    _, l0, acc0, _, l1, acc1 = block(qi, carry, True)
    o0 = acc0 / l0
    o1 = acc1 / l1
    if fox:
        o_ref[0] = jnp.where(lane < HEAD_DIM, o0, o1).astype(BF16)
    else:
        o = o0 - _diff_lambda(dl_ref[...], lam_init) * o1
        ms = jnp.mean(o * o, axis=1, keepdims=True)
        o_ref[0] = (o * lax.rsqrt(ms + EPS) * sg_ref[...] * (1.0 - lam_init)).astype(BF16)


def _prompt_attention(q, k, v, extra, *, fox, lam_init=0.0):
    b, s, _ = q.shape
    tq = tk = min(512, s)
    nq = s // tq
    qspec = pl.BlockSpec((1, tq, LANES), lambda bi, c, qi: (bi, qi, c))
    kspec = pl.BlockSpec((1, s, LANES), lambda bi, c, qi: (bi, 0, c))
    if fox:
        (fcum,) = extra
        espec = [pl.BlockSpec((None, None, nq, 2, tk), lambda bi, c, qi: (c, bi, 0, 0, 0))]
    else:
        dl, sg = extra
        espec = [_full(dl.shape), _full(sg.shape)]
    return pl.pallas_call(
        functools.partial(_attn_kernel, fox=fox, tq=tq, tk=tk, lam_init=lam_init),
        grid=(b, ATT_W // LANES, nq),
        in_specs=[qspec, kspec, kspec] + espec,
        out_specs=qspec,
        out_shape=jax.ShapeDtypeStruct((b, s, ATT_W), BF16),
        compiler_params=_cparams("parallel", "parallel", "arbitrary"),
    )(q, k, v, *extra)


PAGES_PER_STEP = 16
N_ROWS = 32
N_NEW = 128
DIFF_NEW_KEYS = 32


def _decode_kernel(*refs, fox, n_q, lam_init):
    pps = PAGES_PER_STEP
    pos = 1
    q_ref = refs[pos]; pos += 1
    k_refs = refs[pos:pos + pps]; pos += pps
    v_refs = refs[pos:pos + pps]; pos += pps
    if fox:
        f_refs = refs[pos:pos + pps]; pos += pps
        kn_ref, vn_ref, maskn_ref, fn_ref, tri_ref = refs[pos:pos + 5]; pos += 5
    else:
        kn_ref, vn_ref, mask_ref, maskn_ref, dl_ref, sg_ref = refs[pos:pos + 6]; pos += 6
    o_ref = refs[pos]; pos += 1
    q_sc, m_sc, l_sc, acc_sc, car_sc = refs[pos:pos + 5]
    j = pl.program_id(1)

    @pl.when(j == 0)
    def _():
        q = q_ref[0]
        if fox:
            head = lax.broadcasted_iota(jnp.int32, (FOX_HEADS, ATT_W), 0)
            col = lax.broadcasted_iota(jnp.int32, (FOX_HEADS, ATT_W), 1)
            own = (col // HEAD_DIM) == head
            for qq in range(n_q):
                row = jnp.broadcast_to(q[qq:qq + 1, :], (FOX_HEADS, ATT_W))
                q_sc[qq * FOX_HEADS:(qq + 1) * FOX_HEADS, :] = jnp.where(own, row, 0.0).astype(BF16)
        else:
            lane = lax.broadcasted_iota(jnp.int32, (1, LANES), 1)
            q_sc[...] = jnp.concatenate([jnp.where(lane < HEAD_DIM, q, 0.0),
                                         jnp.where(lane >= HEAD_DIM, q, 0.0)], axis=0).astype(BF16)
        m_sc[...] = jnp.full(m_sc.shape, NEG_INF, F32)
        l_sc[...] = jnp.zeros(l_sc.shape, F32)
        acc_sc[...] = jnp.zeros(acc_sc.shape, F32)
        car_sc[...] = jnp.zeros(car_sc.shape, F32)

    def pages(k_pages, v_pages, f_pages, mask):
        scores = []
        if fox:
            carry = car_sc[...]
        for u, k_page in enumerate(k_pages):
            if fox:
                s = _dot(q_sc[...], k_page.astype(BF16))
                fe = _dot_sel_r(f_pages[u], tri_ref[...]) + carry
                carry = fe[:, PAGE - 1:PAGE]
                s = s - jnp.concatenate([fe * LOG2E] * n_q, axis=0)
            else:
                s = _dot_nt(q_sc[...], k_page.astype(BF16))
            scores.append(s if mask is None else s + mask)
        if fox:
            car_sc[...] = carry
        m_prev = m_sc[...]
        m_new = jnp.maximum(m_prev, jnp.max(functools.reduce(jnp.maximum, scores), axis=1, keepdims=True))
        alpha = jnp.exp2(m_prev - m_new)
        p_sum = None
        pv = None
        for s, v_page in zip(scores, v_pages):
            p = jnp.exp2(s - m_new)
            p_sum = p if p_sum is None else p_sum + p
            p = p.astype(BF16)
            t = _dot_nt(p, v_page.astype(BF16)) if fox else _dot(p, v_page.astype(BF16))
            pv = t if pv is None else pv + t
        l_sc[...] = alpha * l_sc[...] + jnp.sum(p_sum, axis=1, keepdims=True)
        acc_sc[...] = alpha * acc_sc[...] + pv
        m_sc[...] = m_new

    pages([r[...] for r in k_refs], [r[...] for r in v_refs], [r[...] for r in f_refs] if fox else None,
          None if fox else mask_ref[...])

    @pl.when(j == pl.num_programs(1) - 1)
    def _():
        pages([kn_ref[0]], [vn_ref[0]], [fn_ref[0]] if fox else None, maskn_ref[...])
        accn = acc_sc[...] / l_sc[...]
        if fox:
            head = lax.broadcasted_iota(jnp.int32, (FOX_HEADS, ATT_W), 0)
            col = lax.broadcasted_iota(jnp.int32, (FOX_HEADS, ATT_W), 1)
            own = jnp.where((col // HEAD_DIM) == head, 1.0, 0.0)
            o_ref[0] = jnp.concatenate(
                [jnp.sum(accn[qq * FOX_HEADS:(qq + 1) * FOX_HEADS, :] * own, axis=0, keepdims=True)
                 for qq in range(n_q)], axis=0)
        else:
            half = N_ROWS // 2
            o = accn[0:half, :] - _diff_lambda(dl_ref[...], lam_init) * accn[half:N_ROWS, :]
            ms = jnp.mean(o * o, axis=1, keepdims=True)
            o_ref[0] = o * lax.rsqrt(ms + EPS) * sg_ref[...] * (1.0 - lam_init)


def _decode_attention(page_table, layer, q, k_pool, v_pool, f_pool, k_new, v_new, f_new, extra, *, fox,
                      lam_init=0.0):
    db, q_rows, q_cols = q.shape
    n_pages = page_table.shape[1]
    pps = PAGES_PER_STEP
    assert n_pages % pps == 0
    page_rows, page_cols = k_pool.shape[2], k_pool.shape[3]
    const = lambda a: pl.BlockSpec(a.shape, lambda b, j, pt: (0,) * a.ndim)

    def pool_spec(u, rows, cols):
        return pl.BlockSpec((None, None, rows, cols), lambda b, j, pt: (layer, pt[b, j * pps + u], 0, 0))

    per_seq = lambda a: pl.BlockSpec((1,) + a.shape[1:], lambda b, j, pt: (b, 0, 0))
    in_specs = [per_seq(q)]
    args = [q]
    in_specs += [pool_spec(u, page_rows, page_cols) for u in range(pps)]
    args += [k_pool] * pps
    in_specs += [pool_spec(u, page_rows, page_cols) for u in range(pps)]
    args += [v_pool] * pps
    if fox:
        in_specs += [pool_spec(u, FOX_HEADS, PAGE) for u in range(pps)]
        args += [f_pool] * pps
        mask_new, tri = extra
        in_specs += [per_seq(k_new), per_seq(v_new), const(mask_new), per_seq(f_new), const(tri)]
        args += [k_new, v_new, mask_new, f_new, tri]
        n_q, acc_cols = q_rows, ATT_W
    else:
        in_specs += [per_seq(k_new), per_seq(v_new)] + [const(a) for a in extra]
        args += [k_new, v_new, *extra]
        n_q, acc_cols = q_rows // DIFF_HEADS, q_cols
    grid_spec = pltpu.PrefetchScalarGridSpec(
        num_scalar_prefetch=1,
        grid=(db, n_pages // pps),
        in_specs=in_specs,
        out_specs=pl.BlockSpec((1, q_rows, q_cols), lambda b, j, pt: (b, 0, 0)),
        scratch_shapes=[pltpu.VMEM((N_ROWS, acc_cols), BF16), pltpu.VMEM((N_ROWS, 1), F32),
                        pltpu.VMEM((N_ROWS, 1), F32), pltpu.VMEM((N_ROWS, acc_cols), F32),
                        pltpu.VMEM((FOX_HEADS, 1), F32)],
    )
    return pl.pallas_call(
        functools.partial(_decode_kernel, fox=fox, n_q=n_q, lam_init=lam_init),
        grid_spec=grid_spec,
        out_shape=jax.ShapeDtypeStruct((db, q_rows, q_cols), F32),
        compiler_params=_cparams("parallel", "arbitrary"),
    )(page_table, *args)


def _ssd_kernel(*refs, has_init):
    if has_init:
        (xbc_ref, z_ref, dt_ref, conv0_ref, s0_ref, cw_ref, cb_ref, alog_ref, dexp_ref, ng_ref, exp_ref, tri_ref,
         y_ref, s_out, xbuf, st_sc) = refs
    else:
        (xbc_ref, z_ref, dt_ref, conv0_ref, cw_ref, cb_ref, alog_ref, dexp_ref, ng_ref, exp_ref, tri_ref,
         y_ref, s_out, xbuf, st_sc) = refs
    L = SSD_CHUNK
    c = pl.program_id(1)

    @pl.when(c == 0)
    def _():
        xbuf[0:SUBLANES, :] = conv0_ref[0]
        if has_init:
            st_sc[...] = jnp.transpose(s0_ref[0].reshape(SSD_INNER, SSD_STATE))
        else:
            st_sc[...] = jnp.zeros(st_sc.shape, F32)

    xbuf[SUBLANES:SUBLANES + L, :] = xbc_ref[0]
    conv = cb_ref[...] + cw_ref[3:4, :] * xbuf[SUBLANES:SUBLANES + L, :]
    for kk in range(1, SSD_CONV):
        conv = conv + cw_ref[3 - kk:4 - kk, :] * xbuf[SUBLANES - kk:SUBLANES - kk + L, :]
    xbuf[0:SUBLANES, :] = xbuf[L:L + SUBLANES, :]
    xc = _silu(conv)
    xs = xc[:, 0:SSD_INNER]
    gw = SSD_GROUPS * SSD_STATE
    bm = xc[:, SSD_INNER:SSD_INNER + gw]
    cm = xc[:, SSD_INNER + gw:SSD_INNER + 2 * gw]

    dt = dt_ref[0]
    a = dt * (-jnp.exp(alog_ref[...]))
    acum = _dot_sel_l(tri_ref[...], a)
    acum_t = jnp.transpose(acum)
    total = acum[L - 1:L, :]
    expand = exp_ref[...]
    dt_e = _dot_sel_r(dt, expand)
    eacum_e = _dot_sel_r(jnp.exp(acum), expand)
    toend_e = _dot_sel_r(jnp.exp(total - acum), expand)
    etotal_e = _dot_sel_r(jnp.exp(total), expand)
    xd = xs * dt_e
    xw = (xd * toend_e).astype(BF16)
    xd_b = xd.astype(BF16)

    ii = lax.broadcasted_iota(jnp.int32, (L, L), 0)
    jj = lax.broadcasted_iota(jnp.int32, (L, L), 1)
    lower = jj <= ii
    lane = lax.broadcasted_iota(jnp.int32, (1, LANES), 1)
    heads_per_group = SSD_HEADS // SSD_GROUPS
    gcols = heads_per_group * SSD_HEAD_DIM
    y_parts = []
    for g in range(SSD_GROUPS):
        bg = bm[:, g * SSD_STATE:(g + 1) * SSD_STATE]
        cg = cm[:, g * SSD_STATE:(g + 1) * SSD_STATE].astype(BF16)
        bg_t = jnp.transpose(bg).astype(BF16)
        cb = _dot(cg, bg_t)
        st_g = st_sc[:, g * gcols:(g + 1) * gcols]
        y_off = _dot(cg, st_g.astype(BF16)) * eacum_e[:, g * gcols:(g + 1) * gcols]
        st_sc[:, g * gcols:(g + 1) * gcols] = (st_g * etotal_e[:, g * gcols:(g + 1) * gcols]
                                               + _dot(bg_t, xw[:, g * gcols:(g + 1) * gcols]))
        for pr in range(heads_per_group // 2):
            h0 = g * heads_per_group + 2 * pr
            mats = []
            for hh in (h0, h0 + 1):
                seg = acum[:, hh:hh + 1] - acum_t[hh:hh + 1, :]
                mats.append((cb * jnp.exp(jnp.where(lower, seg, NEG_INF))).astype(BF16))
            xp = xd_b[:, h0 * SSD_HEAD_DIM:(h0 + 2) * SSD_HEAD_DIM]
            zero = jnp.zeros_like(xp)
            stacked = jnp.concatenate([jnp.where(lane < SSD_HEAD_DIM, xp, zero),
                                       jnp.where(lane >= SSD_HEAD_DIM, xp, zero)], axis=0)
            y_diag = _dot(jnp.concatenate(mats, axis=1), stacked)
            y_parts.append(y_diag + y_off[:, 2 * pr * SSD_HEAD_DIM:(2 * pr + 2) * SSD_HEAD_DIM])
    y = jnp.concatenate(y_parts, axis=1) + dexp_ref[...] * xs
    y = y * _silu(z_ref[0])
    outs = []
    for g in range(SSD_GROUPS):
        yg = y[:, g * gcols:(g + 1) * gcols]
        ms = jnp.mean(yg * yg, axis=1, keepdims=True)
        outs.append(yg * lax.rsqrt(ms + EPS) * ng_ref[:, g * gcols:(g + 1) * gcols])
    y_ref[0] = jnp.concatenate(outs, axis=1).astype(BF16)

    @pl.when(c == pl.num_programs(1) - 1)
    def _():
        s_out[0] = jnp.transpose(st_sc[...]).reshape(SSD_HEADS, SSD_HEAD_DIM, SSD_STATE)


def _ssd(xbc, z, dt, conv0, s0, cw, cb, alog, dexp, ng, expand, tri):
    b, s, _ = xbc.shape
    L = SSD_CHUNK
    has_init = s0 is not None
    blk = lambda n: pl.BlockSpec((1, L, n), lambda bi, c: (bi, c, 0))
    per_b = lambda *shape: pl.BlockSpec((1,) + shape, lambda bi, c: (bi,) + (0,) * len(shape))
    in_specs = [blk(SSD_CONV_DIM), blk(SSD_INNER), blk(LANES), per_b(SUBLANES, SSD_CONV_DIM)]
    args = [xbc, z, dt, conv0]
    if has_init:
        in_specs.append(per_b(SSD_HEADS, SSD_HEAD_DIM, SSD_STATE))
        args.append(s0)
    consts = [cw, cb, alog, dexp, ng, expand, tri]
    in_specs += [_full(a.shape) for a in consts]
    args += consts
    return pl.pallas_call(
        functools.partial(_ssd_kernel, has_init=has_init),
        grid=(b, s // L),
        in_specs=in_specs,
        out_specs=[blk(SSD_INNER), per_b(SSD_HEADS, SSD_HEAD_DIM, SSD_STATE)],
        out_shape=[jax.ShapeDtypeStruct((b, s, SSD_INNER), BF16),
                   jax.ShapeDtypeStruct((b, SSD_HEADS, SSD_HEAD_DIM, SSD_STATE), F32)],
        scratch_shapes=[pltpu.VMEM((L + SUBLANES, SSD_CONV_DIM), F32), pltpu.VMEM((SSD_STATE, SSD_INNER), F32)],
        compiler_params=_cparams("parallel", "arbitrary"),
    )(*args)


def _merge_kernel(x_ref, sc_ref, sh_ref, g1_ref, ng_ref, a_ref, b_ref, d_ref, wg_ref, wa_ref, wb_ref, wd_ref,
                  wo_ref, o_ref):
    x = x_ref[...]
    h = _rms_mod(x, ng_ref[...], sc_ref[0], sh_ref[0]).astype(BF16)
    m = _sigmoid(_dot(h, wg_ref[:, 0:D_MODEL])) * _dot(a_ref[...], wa_ref[...])
    m = m + _sigmoid(_dot(h, wg_ref[:, D_MODEL:2 * D_MODEL])) * _dot(b_ref[...], wb_ref[...])
    m = m + _sigmoid(_dot(h, wg_ref[:, 2 * D_MODEL:3 * D_MODEL])) * _dot(d_ref[...], wd_ref[...])
    o_ref[...] = x + g1_ref[0] * _dot(m.astype(BF16), wo_ref[...])


def _merge(x2, sc, sh, g1, ng, o_fox, y_ssd, o_diff, wg, wa, wb, wd, wo, *, tm, rows_per_group):
    t = x2.shape[0]
    mod_spec = _mod_specs(tm, rows_per_group, sc.shape[1])
    row = lambda n: pl.BlockSpec((tm, n), lambda i: (i, 0))
    return pl.pallas_call(
        _merge_kernel,
        grid=(t // tm,),
        in_specs=[row(D_MODEL), mod_spec, mod_spec, mod_spec, _full((1, D_MODEL)), row(ATT_W), row(SSD_INNER),
                  row(ATT_W), _full(wg.shape), _full(wa.shape), _full(wb.shape), _full(wd.shape), _full(wo.shape)],
        out_specs=row(D_MODEL),
        out_shape=jax.ShapeDtypeStruct((t, D_MODEL), F32),
        compiler_params=_cparams("parallel"),
    )(x2, sc, sh, g1, ng, o_fox, y_ssd, o_diff, wg, wa, wb, wd, wo)


def _top16(x, ids):
    vals, idxs = [], []
    for _ in range(PEER_TOPK):
        m = jnp.max(x, axis=0, keepdims=True)
        idx = jnp.min(jnp.where(x == m, ids, float(2 ** 30)), axis=0, keepdims=True)
        vals.append(m)
        idxs.append(idx)
        x = jnp.where(ids == idx, NEG_INF, x)
    return jnp.concatenate(vals, axis=0), jnp.concatenate(idxs, axis=0)


PAIR_COUNTS = tuple(PEER_TOPK // (i + 1) for i in range(PEER_TOPK))
PAIR_ROWS = -(-sum(PAIR_COUNTS) // SUBLANES) * SUBLANES


def _pair_candidates(v1, v2):
    n = v1.shape[1]
    vals, ids = [], []
    for i, cnt in enumerate(PAIR_COUNTS):
        vals.append(v1[i:i + 1, :] + v2[0:cnt, :])
        ids.append((i * PEER_TOPK + lax.broadcasted_iota(jnp.int32, (cnt, n), 0)).astype(F32))
    pad = PAIR_ROWS - sum(PAIR_COUNTS)
    vals.append(jnp.full((pad, n), NEG_INF, F32))
    ids.append(jnp.full((pad, n), float(2 ** 20), F32))
    return jnp.concatenate(vals, axis=0), jnp.concatenate(ids, axis=0)


def _pick_rows(table, sel):
    out = jnp.zeros(sel.shape, F32)
    for kk in range(PEER_TOPK):
        out = jnp.where(sel == kk, table[kk:kk + 1, :], out)
    return out


GROUP = 16


def _transpose8(xs, sublane):
    for d in (4, 2, 1):
        keep = lax.bitwise_and(sublane, d) == 0
        ys = list(xs)
        for i in range(SUBLANES):
            if i & d == 0:
                a, b = xs[i], xs[i + d]
                ys[i] = jnp.where(keep, a, pltpu.roll(b, d, 0))
                ys[i + d] = jnp.where(keep, pltpu.roll(a, SUBLANES - d, 0), b)
        xs = ys
    return xs


def _route_kernel(x_ref, sc_ref, sh_ref, ng_ref, wq_ref, keys_ref, h_out, g_out, i1_sc, i2_sc, gw_sc, gs_sc, *, tm):
    h = _rms_mod(x_ref[...], ng_ref[...], sc_ref[0], sh_ref[0]).astype(BF16)
    h_out[...] = h
    key_ids = lax.broadcasted_iota(jnp.int32, (PEER_KEYS, tm), 0).astype(F32)
    i1_rows, i2_rows, gw_rows = [], [], []
    for hd in range(PEER_HEADS):
        tops = []
        for i in range(2):
            lo = (hd * 2 + i) * PEER_KEYS
            qh = _dot(h, wq_ref[:, lo:lo + PEER_KEYS]).astype(BF16)
            tops.append(_top16(_dot_nt(keys_ref[i], qh), key_ids))
        (v1, x1), (v2, x2) = tops
        best_v, best_c = _top16(*_pair_candidates(v1, v2))
        best_c = best_c.astype(jnp.int32)
        i1_rows.append(_pick_rows(x1, lax.shift_right_logical(best_c, 4)))
        i2_rows.append(_pick_rows(x2, lax.bitwise_and(best_c, PEER_TOPK - 1)))
        e = jnp.exp(best_v - best_v[0:1, :])
        gw_rows.append(e / jnp.sum(e, axis=0, keepdims=True))
    i1_sc[...] = jnp.transpose(jnp.concatenate(i1_rows, axis=0))
    i2_sc[...] = jnp.transpose(jnp.concatenate(i2_rows, axis=0))
    gw_sc[...] = jnp.transpose(jnp.concatenate(gw_rows, axis=0))

    key_id = lax.broadcasted_iota(jnp.int32, (PEER_KEYS, LANES), 0).astype(F32)

    sublane = lax.broadcasted_iota(jnp.int32, (SUBLANES, LANES), 0)

    def per_group(gi, carry):
        base = pl.multiple_of(gi * GROUP, GROUP)

        def per_token(tt, c):
            r1 = i1_sc[pl.ds(base + tt, 1), :]
            r2 = i2_sc[pl.ds(base + tt, 1), :]
            rg = gw_sc[pl.ds(base + tt, 1), :]
            a = jnp.where(key_id == r1, rg, 0.0).astype(BF16)
            bsel = jnp.where(key_id == r2, 1.0, 0.0).astype(BF16)
            gs_sc[pl.ds(pl.multiple_of(tt * PEER_KEYS, PEER_KEYS), PEER_KEYS), :] = _dot_nt(a, bsel)
            return c

        lax.fori_loop(0, GROUP, per_token, 0, unroll=8)
        for blk in range(PEER_KEYS // SUBLANES):
            halves = []
            for hf in range(GROUP // SUBLANES):
                rows = [(hf * SUBLANES + i) * PEER_KEYS + blk * SUBLANES for i in range(SUBLANES)]
                halves.append(_transpose8([gs_sc[r:r + SUBLANES, :] for r in rows], sublane))
            for j in range(SUBLANES):
                g_out[gi, blk * SUBLANES + j] = jnp.concatenate([hv[j] for hv in halves], axis=0).astype(BF16)
        return carry

    lax.fori_loop(0, tm // GROUP, per_group, 0)


def _route(x2, sc, sh, ng, wq, keys, *, tm, rows_per_group):
    t = x2.shape[0]
    mod_spec = _mod_specs(tm, rows_per_group, sc.shape[1])
    row = lambda n: pl.BlockSpec((tm, n), lambda i: (i, 0))
    return pl.pallas_call(
        functools.partial(_route_kernel, tm=tm),
        grid=(t // tm,),
        in_specs=[row(D_MODEL), mod_spec, mod_spec, _full((1, D_MODEL)), _full(wq.shape), _full(keys.shape)],
        out_specs=[row(D_MODEL),
                   pl.BlockSpec((tm // GROUP, PEER_KEYS, GROUP, PEER_KEYS), lambda i: (i, 0, 0, 0))],
        out_shape=[jax.ShapeDtypeStruct((t, D_MODEL), BF16),
                   jax.ShapeDtypeStruct((t // GROUP, PEER_KEYS, GROUP, PEER_KEYS), BF16)],
        scratch_shapes=[pltpu.VMEM((tm, LANES), F32)] * 3 + [pltpu.VMEM((GROUP * PEER_KEYS, PEER_KEYS), F32)],
        compiler_params=_cparams("parallel"),
    )(x2, sc, sh, ng, wq, keys)


def _gelu(a):
    return 0.5 * a * (1.0 + lax.erf(a * (2.0 ** -0.5)))


def _expert_kernel(h_ref, g_ref, u_ref, v_ref, x_ref, g2_ref, o_ref, acc_sc):
    j = pl.program_id(1)

    @pl.when(j == 0)
    def _():
        acc_sc[...] = jnp.zeros(acc_sc.shape, F32)

    a = _dot_nt(h_ref[...], u_ref[...])
    tm = a.shape[0]
    parts = []
    for s in range(g_ref.shape[1]):
        g = g_ref[:, s].reshape(tm, PEER_KEYS).astype(F32)
        parts.append((g * _gelu(a[:, s * PEER_KEYS:(s + 1) * PEER_KEYS])).astype(BF16))
    acc_sc[...] += _dot(jnp.concatenate(parts, axis=1), v_ref[...])

    @pl.when(j == pl.num_programs(1) - 1)
    def _():
        o_ref[...] = x_ref[...] + g2_ref[0] * acc_sc[...]


def _experts(h2, gates, u, v, x2, g2, *, tm, rows_per_group, ec=1024):
    t = h2.shape[0]
    tiles_per_group = rows_per_group // tm
    if g2.shape[1] == 1:
        mod_spec = pl.BlockSpec((1, 1, D_MODEL), lambda i, j: (i // tiles_per_group, 0, 0))
    else:
        mod_spec = pl.BlockSpec((1, tm, D_MODEL), lambda i, j: (i // tiles_per_group, i % tiles_per_group, 0))
    row = pl.BlockSpec((tm, D_MODEL), lambda i, j: (i, 0))
    return pl.pallas_call(
        _expert_kernel,
        grid=(t // tm, PEER_EXPERTS // ec),
        in_specs=[row, pl.BlockSpec((tm // GROUP, ec // PEER_KEYS, GROUP, PEER_KEYS), lambda i, j: (i, j, 0, 0)),
                  pl.BlockSpec((ec, D_MODEL), lambda i, j: (j, 0)),
                  pl.BlockSpec((ec, D_MODEL), lambda i, j: (j, 0)), row, mod_spec],
        out_specs=row,
        out_shape=jax.ShapeDtypeStruct((t, D_MODEL), F32),
        scratch_shapes=[pltpu.VMEM((tm, D_MODEL), F32)],
        compiler_params=_cparams("parallel", "arbitrary"),
    )(h2, gates, u, v, x2, g2)


def _consts():
    r = np.arange(ATT_W)
    seg = (r[:, None] // HEAD_DIM == r[None, :] // HEAD_DIM).astype(np.float32)
    i = np.arange(LANES)
    tri_incl = (i[:, None] <= i[None, :]).astype(np.float32)
    tri_time = (i[None, :] <= i[:, None]).astype(np.float32)
    expand = (np.arange(SSD_INNER)[None, :] // SSD_HEAD_DIM == i[:, None]).astype(np.float32)
    return (jnp.asarray(seg, BF16), jnp.asarray(tri_incl, BF16), jnp.asarray(tri_time, BF16),
            jnp.asarray(expand, BF16))


def _decode_consts(n_q):
    to_add = lambda ok: jnp.asarray(np.where(ok, 0.0, -np.inf).astype(np.float32))
    r = np.arange(N_ROWS)[:, None]
    key = np.arange(N_NEW)[None, :]
    fox_new = to_add((key <= r // FOX_HEADS) & (key < n_q))
    head, query = r % DIFF_HEADS, (r % (N_ROWS // 2)) // DIFF_HEADS
    c = np.arange(PAGE * DIFF_HEADS)[None, :]
    diff_past = to_add(c % DIFF_HEADS == head)
    cn = np.arange(DIFF_NEW_KEYS * DIFF_HEADS)[None, :]
    kn = cn // DIFF_HEADS
    diff_new = to_add((cn % DIFF_HEADS == head) & (kn <= query) & (kn < n_q))
    return fox_new, diff_past, diff_new


def _rope_tables(positions):
    half = HEAD_DIM // 2
    inv = ROPE_THETA ** (-jnp.arange(half, dtype=F32) / half)
    ang = positions.astype(F32)[:, None] * inv[None, :]
    cos, sin = jnp.cos(ang), jnp.sin(ang)
    cos64 = jnp.concatenate([cos, cos], axis=1)
    sin64 = jnp.concatenate([-sin, sin], axis=1)
    return jnp.concatenate([cos64, cos64], axis=1), jnp.concatenate([sin64, sin64], axis=1)


def _pad_cols(a, n):
    return jnp.pad(a, ((0, 0), (0, n - a.shape[1])))


def _layer_weights(p, l):
    w_in = p['w_in'][l]
    o = IN_OFF
    col = lambda i: w_in[:, o[i]:o[i + 1]]
    w_fox = jnp.concatenate([col(0), col(1), col(2), _pad_cols(col(3), LANES)], axis=1).astype(BF16)
    wft = jnp.pad(col(3).T, ((0, 16 - FOX_HEADS), (0, 0))).astype(BF16)
    w_ssd = jnp.concatenate([col(4), col(5), _pad_cols(col(6), LANES)], axis=1).astype(BF16)
    w_diff = jnp.concatenate([col(7), col(8), col(9)], axis=1).astype(BF16)
    tile8 = lambda g: jnp.tile(g, ATT_W // HEAD_DIM)[None, :]
    return dict(
        w_fox=w_fox, wft=wft, w_ssd=w_ssd, w_diff=w_diff, w_gate=col(10).astype(BF16),
        fb=p['fox_f_b'][l][None, :], fbt=p['fox_f_b'][l][:, None],
        fox_qg=tile8(p['fox_qn_g'][l]), fox_kg=tile8(p['fox_kn_g'][l]),
        diff_qg=tile8(p['diff_qn_g'][l]), diff_kg=tile8(p['diff_kn_g'][l]),
        dtb=_pad_cols(p['dt_bias'][l][None, :], LANES),
        alog=jnp.pad(p['a_log'][l][None, :], ((0, 0), (0, LANES - SSD_HEADS)), constant_values=NEG_INF),
        dexp=jnp.repeat(p['ssd_d'][l], SSD_HEAD_DIM)[None, :],
        ssd_ng=p['ssd_norm_g'][l][None, :],
        conv_w=p['conv_w'][l], conv_b=p['conv_b'][l][None, :],
        dl=p['diff_lam'][l], sg=p['diff_subln_g'][l][None, :],
        norm1=p['norm1_g'][l][None, :], norm2=p['norm2_g'][l][None, :],
        wa=p['w_br_fox'][l].astype(BF16), wb=p['w_br_ssd'][l].astype(BF16), wd=p['w_br_diff'][l].astype(BF16),
        wo=p['w_out'][l].astype(BF16), wq=p['peer_wq'][l].astype(BF16), keys=p['peer_keys'][l].astype(BF16),
        u=p['peer_u'][l].astype(BF16), v=p['peer_v'][l].astype(BF16),
    )


def _layer(x, mods, w, consts, layer_idx, past):
    seg, tri_incl, tri_time, expand = consts
    b, s, _ = x.shape
    t = b * s
    sh1, sc1, g1, sh2, sc2, g2 = mods
    x2 = x.reshape(t, D_MODEL)
    if past is None:
        rows_per_group = s
        tm = min(512, s)
        p0 = 0
    else:
        rows_per_group = t
        tm = t
        p0 = past['page_table'].shape[1] * PAGE
    kw = dict(tm=tm, rows_per_group=rows_per_group)
    lam_init = 0.8 - 0.6 * math.exp(-0.3 * layer_idx)

    q_f, kf32, kfb, vf32, vfb, logf, logf_t = _fox_proj(
        x2, sc1, sh1, w['norm1'], w['w_fox'], w['wft'], w['fb'], w['fbt'], w['fox_qg'], w['fox_kg'], seg, **kw)
    z, xbc, dt = _ssd_proj(x2, sc1, sh1, w['norm1'], w['w_ssd'], w['dtb'], **kw)
    pos = p0 + jnp.arange(s, dtype=jnp.int32)
    if past is not None:
        pos = jnp.tile(pos, b)
    cos, sin = _rope_tables(pos)
    q_d, kd32, kdb, vd32, vdb = _diff_proj(
        x2, sc1, sh1, w['norm1'], w['w_diff'], w['diff_qg'], w['diff_kg'], seg, cos, sin, **kw)

    if past is None:
        tk = min(512, s)
        fcum = _cumsum_rows(logf_t, tri_incl, s).reshape(FOX_HEADS // 2, 2, b, s // tk, tk).transpose(0, 2, 3, 1, 4)
        o_fox =_prompt_attention(q_f.reshape(b, s, ATT_W), kfb.reshape(b, s, ATT_W), vfb.reshape(b, s, ATT_W),
                                  (fcum,), fox=True)
        o_diff = _prompt_attention(q_d.reshape(b, s, ATT_W), kdb.reshape(b, s, ATT_W), vdb.reshape(b, s, ATT_W),
                                   (w['dl'], w['sg']), fox=False, lam_init=lam_init)
        y, new_ssm = _ssd(xbc.reshape(b, s, -1), z.reshape(b, s, -1), dt.reshape(b, s, -1),
                          jnp.zeros((b, SUBLANES, SSD_CONV_DIM), F32), None,
                          w['conv_w'], w['conv_b'], w['alog'], w['dexp'], w['ssd_ng'], expand, tri_time)
        new_conv = xbc.reshape(b, s, -1)[:, s - (SSD_CONV - 1):, :]
    else:
        pad_rows = lambda a: jnp.pad(a.reshape(b, s, -1), ((0, 0), (0, PAGE - s), (0, 0)))
        f_new = jnp.pad(logf_t.reshape(FOX_HEADS, b, s).transpose(1, 0, 2), ((0, 0), (0, 0), (0, PAGE - s)))
        assert s * FOX_HEADS == N_ROWS
        fox_new, diff_past, diff_new = _decode_consts(s)
        fox_page = lambda a: jnp.pad(a.reshape(b, s, ATT_W).transpose(0, 2, 1), ((0, 0), (0, 0), (0, N_NEW - s)))
        diff_rows = lambda a: a.astype(F32).reshape(b, s * DIFF_HEADS, LANES)
        diff_page = lambda a: jnp.pad(diff_rows(a), ((0, 0), (0, (DIFF_NEW_KEYS - s) * DIFF_HEADS), (0, 0)))
        o_fox = _decode_attention(past['page_table'], layer_idx, q_f.astype(F32).reshape(b, s, ATT_W),
                                  past['fox_k'], past['fox_v'], past['fox_f'], fox_page(kf32), fox_page(vf32),
                                  f_new, (fox_new, tri_incl), fox=True)
        o_diff = _decode_attention(past['page_table'], layer_idx, diff_rows(q_d),
                                   past['diff_k'], past['diff_v'], None, diff_page(kd32), diff_page(vd32),
                                   None, (diff_past, diff_new, w['dl'], w['sg']), fox=False, lam_init=lam_init)
        o_fox = o_fox.astype(BF16)
        o_diff = o_diff.astype(BF16)
        conv0 = jnp.pad(past['conv'][layer_idx], ((0, 0), (SUBLANES - (SSD_CONV - 1), 0), (0, 0)))
        y, new_ssm = _ssd(pad_rows(xbc), pad_rows(z), pad_rows(dt), conv0, past['ssm'][layer_idx],
                          w['conv_w'], w['conv_b'], w['alog'], w['dexp'], w['ssd_ng'], expand, tri_time)
        y = y[:, :s, :]
        xbc_all = jnp.concatenate([past['conv'][layer_idx], xbc.reshape(b, s, -1)], axis=1)
        new_conv = xbc_all[:, -(SSD_CONV - 1):, :]

    x1 = _merge(x2, sc1, sh1, g1, w['norm1'], o_fox.reshape(t, ATT_W), y.reshape(t, SSD_INNER),
                o_diff.reshape(t, ATT_W), w['w_gate'], w['wa'], w['wb'], w['wd'], w['wo'], **kw)

    tm_r = min(256, tm)
    h2, gates = _route(x1, sc2, sh2, w['norm2'], w['wq'], w['keys'], tm=tm_r, rows_per_group=rows_per_group)
    x_out = _experts(h2, gates, w['u'], w['v'], x1, g2, **kw)

    rows = (kf32.reshape(b, s, FOX_HEADS, HEAD_DIM), vf32.reshape(b, s, FOX_HEADS, HEAD_DIM),
            logf.reshape(b, s, FOX_HEADS), kd32.reshape(b, s, DIFF_HEADS, 2 * HEAD_DIM),
            vd32.reshape(b, s, DIFF_HEADS, 2 * HEAD_DIM), new_ssm, new_conv)
    return x_out.reshape(b, s, D_MODEL), rows


def kernel(x_prompt, x_sample, c_prompt, c_sample, cache_fox_k, cache_fox_v, cache_fox_logf, cache_diff_k,
           cache_diff_v, state_ssm, state_conv, page_table, ada_w, ada_b, norm1_g, norm2_g, w_in, fox_f_b,
           fox_qn_g, fox_kn_g, conv_w, conv_b, dt_bias, a_log, ssd_d, ssd_norm_g, diff_qn_g, diff_kn_g,
           diff_lam, diff_subln_g, w_br_fox, w_br_ssd, w_br_diff, w_out, peer_wq, peer_keys, peer_u, peer_v):
    params = dict(w_in=w_in, fox_f_b=fox_f_b, fox_qn_g=fox_qn_g, fox_kn_g=fox_kn_g, conv_w=conv_w, conv_b=conv_b,
                  dt_bias=dt_bias, a_log=a_log, ssd_d=ssd_d, ssd_norm_g=ssd_norm_g, diff_qn_g=diff_qn_g,
                  diff_kn_g=diff_kn_g, diff_lam=diff_lam, diff_subln_g=diff_subln_g, w_br_fox=w_br_fox,
                  w_br_ssd=w_br_ssd, w_br_diff=w_br_diff, w_out=w_out, peer_wq=peer_wq, peer_keys=peer_keys,
                  peer_u=peer_u, peer_v=peer_v, norm1_g=norm1_g, norm2_g=norm2_g)
    depth = w_in.shape[0]
    bp, sp, _ = x_prompt.shape
    bs, ss, _ = x_sample.shape
    consts = _consts()

    n_c = bp + bs
    c_all = jnp.pad(jnp.concatenate([c_prompt, c_sample], axis=0), ((0, (-n_c) % SUBLANES), (0, 0)))
    mod = _ada_mod(c_all, ada_w, ada_b)

    n_pool = cache_fox_k.shape[1]
    past = dict(
        page_table=page_table,
        fox_k=cache_fox_k.transpose(0, 1, 3, 4, 2).reshape(depth, n_pool, ATT_W, PAGE),
        fox_v=cache_fox_v.transpose(0, 1, 3, 4, 2).reshape(depth, n_pool, ATT_W, PAGE),
        fox_f=jnp.swapaxes(cache_fox_logf, 2, 3),
        diff_k=cache_diff_k.reshape(depth, n_pool, PAGE * DIFF_HEADS, 2 * HEAD_DIM),
        diff_v=cache_diff_v.reshape(depth, n_pool, PAGE * DIFF_HEADS, 2 * HEAD_DIM),
        ssm=state_ssm, conv=state_conv)

    xp, xs = x_prompt, x_sample
    rows_p, rows_s = [], []
    for l in range(depth):
        w = _layer_weights(params, l)
        mp = [mod[l, :bp, i * D_MODEL:(i + 1) * D_MODEL][:, None, :] for i in range(6)]
        ms = [jnp.repeat(mod[l, bp:n_c, i * D_MODEL:(i + 1) * D_MODEL], ss, axis=0)[None] for i in range(6)]
        xp, rp = _layer(xp, mp, w, consts, l, None)
        xs, rs = _layer(xs, ms, w, consts, l, past)
        rows_p.append(rp)
        rows_s.append(rs)
    stk = lambda rows, i: jnp.stack([r[i] for r in rows])
    return (xp, xs, stk(rows_p, 0), stk(rows_s, 0), stk(rows_p, 1), stk(rows_s, 1), stk(rows_p, 2), stk(rows_s, 2),
            stk(rows_p, 3), stk(rows_s, 3), stk(rows_p, 4), stk(rows_s, 4), stk(rows_p, 5), stk(rows_s, 5),
            stk(rows_p, 6), stk(rows_s, 6))
```
